```python
import jax
import jax.numpy as jnp
from jax import lax
import numpy as np

D_MODEL = 4096
BATCH = 2
SEQ = 8192
DEPTH = 4

A_HEADS = 4
A_DK = 128
A_DV = 256
B_HEADS = 8
B_DK = 128
B_DV = 128
C_HEADS = 4
C_DK = 128
C_DV = 256
GLA_RANK = 16
GLA_TAU = 16.0
GATE_RANK = 256
D_FF = 4 * D_MODEL
CONV_K = 5
CHUNK = 64
EPS = 1e-6

A_QK = A_HEADS * A_DK
A_V = A_HEADS * A_DV
B_QK = B_HEADS * B_DK
B_V = B_HEADS * B_DV
B_QKV = 2 * B_QK + B_V
C_QK = C_HEADS * C_DK
C_V = C_HEADS * C_DV
PROJ_SIZES = (A_QK, A_QK, A_V, A_V, 4 * A_HEADS,
              B_QKV, B_V, 4 * B_HEADS,
              C_QK, C_QK, C_V, C_V, 2 * GLA_RANK,
              GATE_RANK)
PROJ_WIDTH = sum(PROJ_SIZES)

kernel_name = 'hybrid_bidir_mlstm_gdn_gla_encoder'


def _rmsnorm(x, g):
    xf = x.astype(jnp.float32)
    y = xf * lax.rsqrt(jnp.mean(xf * xf, axis=-1, keepdims=True) + EPS)
    return (y * g.astype(jnp.float32)).astype(x.dtype)


def _head_rmsnorm(h, g):
    bsz, seq, nh, d = h.shape
    y = h * lax.rsqrt(jnp.mean(h * h, axis=-1, keepdims=True) + EPS)
    return y.reshape(bsz, seq, nh * d) * g.astype(jnp.float32)


def _l2norm(h):
    return h * lax.rsqrt(jnp.sum(h * h, axis=-1, keepdims=True) + EPS)


def _heads(a, nh):
    return a.astype(jnp.float32).reshape(a.shape[0], a.shape[1], nh, -1)


def _to_chunks(a):
    bsz, seq, nh = a.shape[:3]
    rest = a.shape[3:]
    a = a.reshape((bsz, seq // CHUNK, CHUNK, nh) + rest)
    return a.transpose((1, 0, 3, 2) + tuple(range(4, a.ndim)))


def _from_chunks(a):
    n, bsz, nh, L, d = a.shape
    return a.transpose(1, 0, 3, 2, 4).reshape(bsz, n * L, nh, d)


def _bidir(fn, shared, gates_f, gates_b):
    rev = lambda a: jnp.flip(a, axis=1)
    y_f = fn(*shared, *gates_f)
    y_b = fn(*[rev(a) for a in shared], *[rev(a) for a in gates_b])
    return y_f + rev(y_b)


def _mlstm_dir(q, k, v, i_pre, log_f):
    bsz, seq, nh, dk = q.shape
    dv = v.shape[-1]
    incl = jnp.tril(jnp.ones((CHUNK, CHUNK), dtype=bool))
    xs = (_to_chunks(q), _to_chunks(k * dk ** -0.5), _to_chunks(v), _to_chunks(i_pre), _to_chunks(log_f))

    def step(carry, inp):
        C, n, m = carry
        q_, k_, v_, i_, lf_ = inp
        F = jnp.cumsum(lf_, axis=-1)
        d_log = jnp.where(incl, F[..., :, None] - F[..., None, :] + i_[..., None, :], -jnp.inf)
        inter_log = F + m[..., None]
        m_t = jnp.maximum(inter_log, jnp.max(d_log, axis=-1))
        s = jnp.einsum('bhtd,bhsd->bhts', q_, k_) * jnp.exp(d_log - m_t[..., None])
        w_inter = jnp.exp(inter_log - m_t)
        num = jnp.einsum('bhts,bhsv->bhtv', s, v_) + w_inter[..., None] * jnp.einsum('bhtd,bhvd->bhtv', q_, C)
        den = jnp.sum(s, axis=-1) + w_inter * jnp.einsum('bhtd,bhd->bht', q_, n)
        h = num / jnp.maximum(jnp.abs(den), jnp.exp(-m_t))[..., None]
        m_new = m_t[..., -1]
        w_k = jnp.exp(F[..., -1:] - F + i_ - m_new[..., None])
        a = jnp.exp(F[..., -1] + m - m_new)
        C = a[..., None, None] * C + jnp.einsum('bhsv,bhsd->bhvd', v_ * w_k[..., None], k_)
        n = a[..., None] * n + jnp.einsum('bhs,bhsd->bhd', w_k, k_)
        return (C, n, m_new), h

    init = (jnp.zeros((bsz, nh, dv, dk), jnp.float32),
            jnp.zeros((bsz, nh, dk), jnp.float32),
            jnp.zeros((bsz, nh), jnp.float32))
    _, h = lax.scan(step, init, xs)
    return _from_chunks(h)


def _gdn_dir(q, k, v, g, beta):
    bsz, seq, nh, dk = q.shape
    dv = v.shape[-1]
    incl = jnp.tril(jnp.ones((CHUNK, CHUNK), dtype=bool))
    strict = jnp.tril(jnp.ones((CHUNK, CHUNK), dtype=bool), k=-1)
    eye = jnp.eye(CHUNK, dtype=jnp.float32)
    qc = _to_chunks(q * dk ** -0.5)
    kc = _to_chunks(k)
    vc = _to_chunks(v)
    G = jnp.cumsum(_to_chunks(g), axis=-1)
    bc = _to_chunks(beta)
    diff = G[..., :, None] - G[..., None, :]
    kk = jnp.einsum('nbhtd,nbhsd->nbhts', kc, kc)
    lower = eye + bc[..., :, None] * kk * jnp.exp(jnp.where(strict, diff, -jnp.inf))
    w = lax.linalg.triangular_solve(lower, (bc * jnp.exp(G))[..., None] * kc,
                                    left_side=True, lower=True, unit_diagonal=True)
    u = lax.linalg.triangular_solve(lower, bc[..., None] * vc,
                                    left_side=True, lower=True, unit_diagonal=True)
    a_qk = jnp.einsum('nbhtd,nbhsd->nbhts', qc, kc) * jnp.exp(jnp.where(incl, diff, -jnp.inf))

    def step(S, inp):
        q_, k_, w_, u_, a_, G_ = inp
        U = u_ - jnp.einsum('bhtd,bhdv->bhtv', w_, S)
        o = jnp.exp(G_)[..., None] * jnp.einsum('bhtd,bhdv->bhtv', q_, S) + jnp.einsum('bhts,bhsv->bhtv', a_, U)
        k_dec = k_ * jnp.exp(G_[..., -1:] - G_)[..., None]
        S = jnp.exp(G_[..., -1])[..., None, None] * S + jnp.einsum('bhsd,bhsv->bhdv', k_dec, U)
        return S, o

    s0 = jnp.zeros((bsz, nh, dk, dv), jnp.float32)
    _, o = lax.scan(step, s0, (qc, kc, w, u, a_qk, G))
    return _from_chunks(o)


def _gla_dir(q, k, v, log_a):
    bsz, seq, nh, dk = q.shape
    dv = v.shape[-1]
    incl = jnp.tril(jnp.ones((CHUNK, CHUNK), dtype=bool))[:, :, None]
    xs = (_to_chunks(q * dk ** -0.5), _to_chunks(k), _to_chunks(v), _to_chunks(log_a))

    def step(S, inp):
        q_, k_, v_, g_ = inp
        b = jnp.cumsum(g_, axis=-2)
        dec = jnp.exp(jnp.where(incl, b[..., :, None, :] - b[..., None, :, :], -jnp.inf))
        att = jnp.einsum('bhtsc,bhsc->bhts', q_[..., :, None, :] * dec, k_)
        o = jnp.einsum('bhts,bhsv->bhtv', att, v_) + jnp.einsum('bhtc,bhcv->bhtv', q_ * jnp.exp(b), S)
        b_last = b[..., -1:, :]
        S = jnp.exp(b_last[..., 0, :])[..., None] * S + jnp.einsum('bhsc,bhsv->bhcv', k_ * jnp.exp(b_last - b), v_)
        return S, o

    s0 = jnp.zeros((bsz, nh, dk, dv), jnp.float32)
    _, o = lax.scan(step, s0, xs)
    return _from_chunks(o)


def _short_conv(u, w, b):
    ch = u.shape[-1]
    y = lax.conv_general_dilated(u, w[:, None, :].astype(u.dtype), window_strides=(1,),
                                 padding=[(CONV_K // 2, CONV_K // 2)],
                                 dimension_numbers=('NWC', 'WIO', 'NWC'), feature_group_count=ch)
    return jax.nn.silu(y + b.astype(u.dtype))


def _gated_branch(y, w_b, gh, w_g, b_g, dtype):
    g = jax.nn.sigmoid((gh @ w_g).astype(jnp.float32) + b_g.astype(jnp.float32))
    return g * (y.astype(dtype) @ w_b).astype(jnp.float32)


def setup_inputs(seed: int = 0) -> dict:
    key = jax.random.key(seed)
    ks = jax.random.split(key, 26)
    nrm = lambda k, s: jax.random.normal(k, s, jnp.float32)
    lin = lambda k, s, fan_in: nrm(k, s) * fan_in ** -0.5
    f_bias = jnp.linspace(3.0, 6.0, A_HEADS, dtype=jnp.float32)
    gate_base = jnp.stack([jnp.zeros_like(f_bias), jnp.zeros_like(f_bias), f_bias, f_bias])
    dt = jnp.exp(jax.random.uniform(ks[8], (DEPTH, 2, B_HEADS), jnp.float32, np.log(1e-3), np.log(1e-1)))
    return {
        'x': nrm(ks[0], (BATCH, SEQ, D_MODEL)),
        'norm1_g': 1.0 + 0.02 * nrm(ks[1], (DEPTH, D_MODEL)),
        'w_in': lin(ks[2], (DEPTH, D_MODEL, PROJ_WIDTH), D_MODEL),
        'conv_w': lin(ks[3], (DEPTH, CONV_K, B_QKV), CONV_K),
        'conv_b': 0.01 * nrm(ks[4], (DEPTH, B_QKV)),
        'mlstm_gate_b': gate_base[None] + 0.01 * nrm(ks[5], (DEPTH, 4, A_HEADS)),
        'mlstm_norm_g': 1.0 + 0.02 * nrm(ks[6], (DEPTH, A_V)),
        'gdn_a_log': jnp.log(jax.random.uniform(ks[7], (DEPTH, 2, B_HEADS), jnp.float32, 1.0, 16.0)),
        'gdn_dt_bias': dt + jnp.log(-jnp.expm1(-dt)),
        'gdn_norm_g': 1.0 + 0.02 * nrm(ks[9], (DEPTH, B_V)),
        'gla_w_gate': lin(ks[10], (DEPTH, 2, GLA_RANK, C_QK), GLA_RANK),
        'gla_b_gate': 0.01 * nrm(ks[11], (DEPTH, 2, C_QK)),
        'gla_norm_g': 1.0 + 0.02 * nrm(ks[12], (DEPTH, C_V)),
        'w_branch_a': lin(ks[13], (DEPTH, A_V, D_MODEL), A_V),
        'w_branch_b': lin(ks[14], (DEPTH, B_V, D_MODEL), B_V),
        'w_branch_c': lin(ks[15], (DEPTH, C_V, D_MODEL), C_V),
        'w_merge_gate': lin(ks[16], (DEPTH, 3, GATE_RANK, D_MODEL), GATE_RANK),
        'b_merge_gate': 0.01 * nrm(ks[17], (DEPTH, 3, D_MODEL)),
        'w_out': lin(ks[18], (DEPTH, D_MODEL, D_MODEL), D_MODEL),
        'norm2_g': 1.0 + 0.02 * nrm(ks[19], (DEPTH, D_MODEL)),
        'w_ff1': lin(ks[20], (DEPTH, D_MODEL, D_FF), D_MODEL),
        'w_ff2': lin(ks[21], (DEPTH, D_FF, D_MODEL), D_FF),
        'final_g': 1.0 + 0.02 * nrm(ks[22], (D_MODEL,)),
    }


def reference(x, norm1_g, w_in, conv_w, conv_b, mlstm_gate_b, mlstm_norm_g,
              gdn_a_log, gdn_dt_bias, gdn_norm_g, gla_w_gate, gla_b_gate, gla_norm_g,
              w_branch_a, w_branch_b, w_branch_c, w_merge_gate, b_merge_gate, w_out,
              norm2_g, w_ff1, w_ff2, final_g):
    f32 = jnp.float32
    bsz, seq, _ = x.shape
    split_at = np.cumsum(PROJ_SIZES)[:-1].tolist()
    for l in range(DEPTH):
        xn = _rmsnorm(x, norm1_g[l])
        proj = jnp.einsum('btd,de->bte', xn, w_in[l])
        (aq, ak, av, ao, agt, bqkv, bz, bgt, cq, ck, cv, cr, clr, gh) = jnp.split(proj, split_at, axis=-1)

        agt = agt.astype(f32).reshape(bsz, seq, 4, A_HEADS) + mlstm_gate_b[l].astype(f32)
        h_a = _bidir(_mlstm_dir,
                     (_heads(aq, A_HEADS), _heads(ak, A_HEADS), _heads(av, A_HEADS)),
                     (agt[:, :, 0], jax.nn.log_sigmoid(agt[:, :, 2])),
                     (agt[:, :, 1], jax.nn.log_sigmoid(agt[:, :, 3])))
        y_a = jax.nn.sigmoid(ao.astype(f32)) * _head_rmsnorm(h_a, mlstm_norm_g[l])

        qkv = _short_conv(bqkv.astype(f32), conv_w[l], conv_b[l])
        bq, bk, bv = jnp.split(qkv, [B_QK, 2 * B_QK], axis=-1)
        bq = _l2norm(_heads(bq, B_HEADS))
        bk = _l2norm(_heads(bk, B_HEADS))
        bv = _heads(bv, B_HEADS)
        bgt = bgt.astype(f32).reshape(bsz, seq, 4, B_HEADS)
        a_rate = jnp.exp(gdn_a_log[l].astype(f32))
        dtb = gdn_dt_bias[l].astype(f32)
        dec_f = -a_rate[0] * jax.nn.softplus(bgt[:, :, 0] + dtb[0])
        dec_b = -a_rate[1] * jax.nn.softplus(bgt[:, :, 1] + dtb[1])
        h_b = _bidir(_gdn_dir, (bq, bk, bv),
                     (dec_f, jax.nn.sigmoid(bgt[:, :, 2])),
                     (dec_b, jax.nn.sigmoid(bgt[:, :, 3])))
        y_b = _head_rmsnorm(h_b, gdn_norm_g[l]) * jax.nn.silu(bz.astype(f32))

        clr = clr.astype(f32).reshape(bsz, seq, 2, GLA_RANK)
        lg = jax.nn.log_sigmoid(jnp.einsum('btnr,nrc->btnc', clr, gla_w_gate[l].astype(f32))
                                + gla_b_gate[l].astype(f32)) / GLA_TAU
        lg_f = lg[:, :, 0].reshape(bsz, seq, C_HEADS, C_DK)
        lg_b = lg[:, :, 1].reshape(bsz, seq, C_HEADS, C_DK)
        h_c = _bidir(_gla_dir, (_heads(cq, C_HEADS), _heads(ck, C_HEADS), _heads(cv, C_HEADS)),
                     (lg_f,), (lg_b,))
        y_c = _head_rmsnorm(h_c, gla_norm_g[l]) * jax.nn.silu(cr.astype(f32))

        mix = (_gated_branch(y_a, w_branch_a[l], gh, w_merge_gate[l, 0], b_merge_gate[l, 0], x.dtype)
               + _gated_branch(y_b, w_branch_b[l], gh, w_merge_gate[l, 1], b_merge_gate[l, 1], x.dtype)
               + _gated_branch(y_c, w_branch_c[l], gh, w_merge_gate[l, 2], b_merge_gate[l, 2], x.dtype))
        x = x + jnp.einsum('btd,de->bte', mix.astype(x.dtype), w_out[l])

        xn = _rmsnorm(x, norm2_g[l])
        hid = jnp.square(jax.nn.relu(jnp.einsum('btd,df->btf', xn, w_ff1[l])))
        x = x + jnp.einsum('btf,fd->btd', hid, w_ff2[l])
    return _rmsnorm(x, final_g)
```

```python
import functools

import numpy as np
import jax
import jax.numpy as jnp
from jax import lax
from jax.experimental import pallas as pl
from jax.experimental.pallas import tpu as pltpu

F32 = jnp.float32
BF16 = jnp.bfloat16

A_HEADS, A_DK, A_DV = 4, 128, 256
B_HEADS, B_DK, B_DV = 8, 128, 128
C_HEADS, C_DK, C_DV = 4, 128, 256
GLA_RANK = 16
GLA_TAU = 16.0
GATE_RANK = 256
CONV_K = 5
CHUNK = 64
EPS = 1e-6

A_QK, A_V = A_HEADS * A_DK, A_HEADS * A_DV
B_QK, B_V = B_HEADS * B_DK, B_HEADS * B_DV
B_QKV = 2 * B_QK + B_V
C_QK, C_V = C_HEADS * C_DK, C_HEADS * C_DV
PROJ_SIZES = (A_QK, A_QK, A_V, A_V, 4 * A_HEADS,
              B_QKV, B_V, 4 * B_HEADS,
              C_QK, C_QK, C_V, C_V, 2 * GLA_RANK,
              GATE_RANK)
_SEG_NAMES = ("aq", "ak", "av", "ao", "agt", "bqkv", "bz", "bgt", "cq", "ck", "cv", "cr", "clr", "gh")

LANES = 128
VMEM_LIMIT_BYTES = 56 * 1024 * 1024

_NEW_ORDER = ("bqkv", "bz", "aq", "ak", "av", "ao", "cq", "ck", "cv", "cr", "gh", "agt", "bgt", "clr")


def _layout():
    sizes = dict(zip(_SEG_NAMES, PROJ_SIZES))
    starts = dict(zip(_SEG_NAMES, np.cumsum((0,) + PROJ_SIZES)[:-1].tolist()))
    off, new_off, pieces = 0, {}, []
    for name in _NEW_ORDER:
        w = sizes[name]
        pad = (-w) % LANES
        new_off[name] = off
        pieces.append((starts[name], w, pad))
        off += w + pad
    return new_off, pieces, off


OFF, _PIECES, PROJ_W = _layout()


def _cparams(sem):
    return pltpu.CompilerParams(dimension_semantics=sem, vmem_limit_bytes=VMEM_LIMIT_BYTES)


def _mm(a, b):
    return jnp.dot(a.astype(BF16), b.astype(BF16), preferred_element_type=F32)


def _mm_nt(a, b):
    return lax.dot_general(a.astype(BF16), b.astype(BF16), (((1,), (1,)), ((), ())),
                           preferred_element_type=F32)


def _mm_tn(a, b):
    return lax.dot_general(a.astype(BF16), b.astype(BF16), (((0,), (0,)), ((), ())),
                           preferred_element_type=F32)


def _split3(x):
    hi = x.astype(BF16)
    r = x - hi.astype(F32)
    mid = r.astype(BF16)
    lo = (r - mid.astype(F32)).astype(BF16)
    return hi, mid, lo


def _mm_exact_lhs(t_bf16, x):
    hi, mid, lo = _split3(x)
    d = functools.partial(jnp.dot, preferred_element_type=F32)
    return d(t_bf16, hi) + d(t_bf16, mid) + d(t_bf16, lo)


def _mm_x3(a, b):
    ah = a.astype(BF16)
    al = (a - ah.astype(F32)).astype(BF16)
    bh = b.astype(BF16)
    bl = (b - bh.astype(F32)).astype(BF16)
    d = functools.partial(jnp.dot, preferred_element_type=F32)
    return d(ah, bh) + d(ah, bl) + d(al, bh)


def _softplus(x):
    return jnp.maximum(x, 0.0) + jnp.log1p(jnp.exp(-jnp.abs(x)))


def _log_sigmoid(x):
    return -_softplus(-x)


def _sigmoid(x):
    return 1.0 / (1.0 + jnp.exp(-x))


def _tri_masks(rev):
    r = lax.broadcasted_iota(jnp.int32, (CHUNK, CHUNK), 0)
    c = lax.broadcasted_iota(jnp.int32, (CHUNK, CHUNK), 1)
    incl = (c >= r) if rev else (c <= r)
    strict = (c > r) if rev else (c < r)
    return incl, strict


def _rmsnorm_kernel(x_ref, g_ref, o_ref):
    x = x_ref[...]
    y = x * lax.rsqrt(jnp.mean(x * x, axis=-1, keepdims=True) + EPS)
    o_ref[...] = (y * g_ref[...]).astype(o_ref.dtype)


def _rmsnorm(x, g, out_dtype, tm=256):
    m, d = x.shape
    return pl.pallas_call(
        _rmsnorm_kernel,
        grid=(m // tm,),
        in_specs=[pl.BlockSpec((tm, d), lambda i: (i, 0)), pl.BlockSpec((1, d), lambda i: (0, 0))],
        out_specs=pl.BlockSpec((tm, d), lambda i: (i, 0)),
        out_shape=jax.ShapeDtypeStruct((m, d), out_dtype),
        compiler_params=_cparams(("parallel",)),
        name="rmsnorm",
    )(x, g.reshape(1, d))


def _mm_fullk_kernel(a_ref, b_ref, *rest, epilogue):
    o_ref = rest[-1]
    acc = jnp.dot(a_ref[...], b_ref[...], preferred_element_type=F32)
    if epilogue == "relu2":
        acc = jnp.square(jnp.maximum(acc, 0.0))
    elif epilogue == "residual":
        acc = acc + rest[0][...]
    o_ref[...] = acc.astype(o_ref.dtype)


def _matmul_fullk(a, b, *, tm, tn, out_dtype, epilogue="none", residual=None, name="matmul"):
    m, k = a.shape
    n = b.shape[1]
    in_specs = [pl.BlockSpec((tm, k), lambda i, j: (i, 0)), pl.BlockSpec((k, tn), lambda i, j: (0, j))]
    args = [a, b]
    if epilogue == "residual":
        in_specs.append(pl.BlockSpec((tm, tn), lambda i, j: (i, j)))
        args.append(residual)
    return pl.pallas_call(
        functools.partial(_mm_fullk_kernel, epilogue=epilogue),
        grid=(m // tm, n // tn),
        in_specs=in_specs,
        out_specs=pl.BlockSpec((tm, tn), lambda i, j: (i, j)),
        out_shape=jax.ShapeDtypeStruct((m, n), out_dtype),
        compiler_params=_cparams(("parallel", "parallel")),
        name=name,
    )(*args)


def _mm_ktiled_res_kernel(a_ref, b_ref, r_ref, o_ref, acc_ref):
    k = pl.program_id(2)

    @pl.when(k == 0)
    def _():
        acc_ref[...] = r_ref[...]

    acc_ref[...] += jnp.dot(a_ref[...], b_ref[...], preferred_element_type=F32)

    @pl.when(k == pl.num_programs(2) - 1)
    def _():
        o_ref[...] = acc_ref[...]


def _matmul_ktiled_residual(a, b, residual, *, tm, tn, tk, name="matmul_k"):
    m, kk = a.shape
    n = b.shape[1]
    return pl.pallas_call(
        _mm_ktiled_res_kernel,
        grid=(m // tm, n // tn, kk // tk),
        in_specs=[pl.BlockSpec((tm, tk), lambda i, j, k: (i, k)),
                  pl.BlockSpec((tk, tn), lambda i, j, k: (k, j)),
                  pl.BlockSpec((tm, tn), lambda i, j, k: (i, j))],
        out_specs=pl.BlockSpec((tm, tn), lambda i, j, k: (i, j)),
        out_shape=jax.ShapeDtypeStruct((m, n), F32),
        scratch_shapes=[pltpu.VMEM((tm, tn), F32)],
        compiler_params=_cparams(("parallel", "parallel", "arbitrary")),
        name=name,
    )(a, b, residual)


A_AUG = A_DV + LANES


def _mlstm_kernel(qf_ref, kf_ref, vf_ref, gf_ref, qb_ref, kb_ref, vb_ref, gb_ref, bias_ref,
                  hf_ref, hb_ref, st_ref, m_ref):
    n = pl.program_id(1)

    @pl.when(n == 0)
    def _():
        st_ref[...] = jnp.zeros_like(st_ref)
        m_ref[...] = jnp.zeros_like(m_ref)

    lane = lax.broadcasted_iota(jnp.int32, (CHUNK, LANES), 1)
    ones_tile = jnp.where(lane == 0, 1.0, 0.0).astype(F32)
    kscale = A_DK ** -0.5

    for d, (q_ref, k_ref, v_ref, g_ref, h_ref) in enumerate(
            ((qf_ref, kf_ref, vf_ref, gf_ref, hf_ref), (qb_ref, kb_ref, vb_ref, gb_ref, hb_ref))):
        rev = d == 1
        incl, _ = _tri_masks(rev)
        cum = jnp.where(incl, 1.0, 0.0).astype(BF16)
        last = 0 if rev else CHUNK - 1
        g = g_ref[...] + bias_ref[...]
        ls = _log_sigmoid(g)
        f_col = _mm_exact_lhs(cum, ls)
        f_row = f_col.T
        g_row = g.T
        for h in range(A_HEADS):
            j = d * A_HEADS + h
            ci = d * A_HEADS + h
            cf = 2 * A_HEADS + d * A_HEADS + h
            q = q_ref[:, h * A_DK:(h + 1) * A_DK]
            k = k_ref[:, h * A_DK:(h + 1) * A_DK] * kscale
            v_aug = jnp.concatenate([v_ref[:, h * A_DV:(h + 1) * A_DV], ones_tile], axis=1)
            fc = f_col[:, cf:cf + 1]
            fr = f_row[cf:cf + 1, :]
            ir = g_row[ci:ci + 1, :]
            ic = g[:, ci:ci + 1]
            m_prev = m_ref[j:j + 1, 0:1]
            st = st_ref[j]

            d_log = jnp.where(incl, fc - fr + ir, -jnp.inf)
            inter = fc + m_prev
            m_t = jnp.maximum(inter, jnp.max(d_log, axis=-1, keepdims=True))
            s = _mm_nt(q, k) * jnp.exp(d_log - m_t)
            w_inter = jnp.exp(inter - m_t)
            num = _mm(s, v_aug) + w_inter * _mm(q, st)
            den = num[:, A_DV:A_DV + 1]
            hh = num[:, :A_DV] / jnp.maximum(jnp.abs(den), jnp.exp(-m_t))
            h_ref[:, h * A_DV:(h + 1) * A_DV] = hh

            m_new = m_t[last:last + 1, :]
            f_last = fc[last:last + 1, :]
            w_k = jnp.exp(f_last - fc + ic - m_new)
            a = jnp.exp(f_last + m_prev - m_new)
            st_ref[j] = a * st + _mm_tn(k, v_aug * w_k)
            m_ref[j:j + 1, :] = jnp.broadcast_to(m_new, (1, LANES))


def _mlstm(proj, gate_bias, bsz, seq):
    nchunk = seq // CHUNK
    fwd = lambda b, n: b * nchunk + n
    bwd = lambda b, n: b * nchunk + (nchunk - 1 - n)

    def specs(row):
        return [pl.BlockSpec((CHUNK, A_QK), lambda b, n: (row(b, n), OFF["aq"] // A_QK)),
                pl.BlockSpec((CHUNK, A_QK), lambda b, n: (row(b, n), OFF["ak"] // A_QK)),
                pl.BlockSpec((CHUNK, A_V), lambda b, n: (row(b, n), OFF["av"] // A_V)),
                pl.BlockSpec((CHUNK, LANES), lambda b, n: (row(b, n), OFF["agt"] // LANES))]

    m = bsz * seq
    return pl.pallas_call(
        _mlstm_kernel,
        grid=(bsz, nchunk),
        in_specs=specs(fwd) + specs(bwd) + [pl.BlockSpec((1, LANES), lambda b, n: (0, 0))],
        out_specs=[pl.BlockSpec((CHUNK, A_V), lambda b, n: (fwd(b, n), 0)),
                   pl.BlockSpec((CHUNK, A_V), lambda b, n: (bwd(b, n), 0))],
        out_shape=[jax.ShapeDtypeStruct((m, A_V), F32)] * 2,
        scratch_shapes=[pltpu.VMEM((2 * A_HEADS, A_DK, A_AUG), F32), pltpu.VMEM((2 * A_HEADS, LANES), F32)],
        compiler_params=_cparams(("parallel", "arbitrary")),
        name="mlstm",
    )(proj, proj, proj, proj, proj, proj, proj, proj, gate_bias)


HALO = 8


def _conv_kernel(cur_ref, prev_ref, nxt_ref, w_ref, b_ref, o_ref, buf_ref):
    i = pl.program_id(1)
    tc = cur_ref.shape[0]
    first = i == 0
    last = i == pl.num_programs(1) - 1
    buf_ref[0:HALO, :] = jnp.where(first, 0.0, prev_ref[...])
    buf_ref[HALO:HALO + tc, :] = cur_ref[...]
    buf_ref[HALO + tc:HALO + tc + HALO, :] = jnp.where(last, 0.0, nxt_ref[...])
    pad = CONV_K // 2
    qscale = B_DK ** -0.5
    for c in range(B_QKV // LANES):
        cols = slice(c * LANES, (c + 1) * LANES)
        y = b_ref[:, cols]
        for j in range(CONV_K):
            y = y + w_ref[j:j + 1, cols] * buf_ref[HALO - pad + j:HALO - pad + j + tc, cols]
        y = y * _sigmoid(y)
        if c < 2 * B_HEADS:
            y = y * lax.rsqrt(jnp.sum(y * y, axis=-1, keepdims=True) + EPS)
            if c < B_HEADS:
                y = y * qscale
        o_ref[:, cols] = y


def _short_conv_qkv(proj, conv_w, conv_b, bsz, seq, tc=256):
    nt = seq // tc
    hb = tc // HALO
    nrow8 = bsz * seq // HALO
    cur = lambda b, i: (b * nt + i, 0)
    prev = lambda b, i: (jnp.maximum((b * nt + i) * hb - 1, 0), 0)
    nxt = lambda b, i: (jnp.minimum((b * nt + i + 1) * hb, nrow8 - 1), 0)
    w8 = jnp.zeros((8, B_QKV), F32).at[:CONV_K].set(conv_w)
    return pl.pallas_call(
        _conv_kernel,
        grid=(bsz, nt),
        in_specs=[pl.BlockSpec((tc, B_QKV), cur), pl.BlockSpec((HALO, B_QKV), prev),
                  pl.BlockSpec((HALO, B_QKV), nxt),
                  pl.BlockSpec((8, B_QKV), lambda b, i: (0, 0)), pl.BlockSpec((1, B_QKV), lambda b, i: (0, 0))],
        out_specs=pl.BlockSpec((tc, B_QKV), cur),
        out_shape=jax.ShapeDtypeStruct((bsz * seq, B_QKV), F32),
        scratch_shapes=[pltpu.VMEM((tc + 2 * HALO, B_QKV), F32)],
        compiler_params=_cparams(("parallel", "arbitrary")),
        name="short_conv",
    )(proj, proj, proj, w8, conv_b.reshape(1, B_QKV))


def _unit_lower_inverse(x_neg):
    r = lax.broadcasted_iota(jnp.int32, (CHUNK, CHUNK), 0)
    c = lax.broadcasted_iota(jnp.int32, (CHUNK, CHUNK), 1)
    p = jnp.where(r == c, 1.0, 0.0).astype(F32) + x_neg
    y = x_neg
    steps = int(np.log2(CHUNK)) - 1
    for it in range(steps):
        y = _mm_x3(y, y)
        p = p + _mm_x3(p, y)
    return p


def _gdn_kernel(qkvf_ref, gf_ref, qkvb_ref, gb_ref, alog_ref, dtb_ref, of_ref, ob_ref, s_ref):
    n = pl.program_id(1)

    @pl.when(n == 0)
    def _():
        s_ref[...] = jnp.zeros_like(s_ref)

    for d, (x_ref, g_ref, o_ref) in enumerate(((qkvf_ref, gf_ref, of_ref), (qkvb_ref, gb_ref, ob_ref))):
        rev = d == 1
        incl, strict = _tri_masks(rev)
        cum = jnp.where(incl, 1.0, 0.0).astype(BF16)
        last = 0 if rev else CHUNK - 1
        gt = g_ref[...]
        dec = -jnp.exp(alog_ref[...]) * _softplus(gt + dtb_ref[...])
        beta = _sigmoid(gt)
        g_col = _mm_exact_lhs(cum, dec)
        g_row = g_col.T
        e_col = jnp.exp(g_col)
        for h in range(B_HEADS):
            j = d * B_HEADS + h
            cd = d * B_HEADS + h
            cb = 2 * B_HEADS + d * B_HEADS + h
            q = x_ref[:, h * B_DK:(h + 1) * B_DK]
            k = x_ref[:, B_QK + h * B_DK:B_QK + (h + 1) * B_DK]
            v = x_ref[:, 2 * B_QK + h * B_DV:2 * B_QK + (h + 1) * B_DV]
            gc = g_col[:, cd:cd + 1]
            gr = g_row[cd:cd + 1, :]
            ec = e_col[:, cd:cd + 1]
            bc = beta[:, cb:cb + 1]
            s = s_ref[j]

            diff = gc - gr
            lower = bc * _mm_nt(k, k) * jnp.exp(jnp.where(strict, diff, -jnp.inf))
            a_qk = _mm_nt(q, k) * jnp.exp(jnp.where(incl, diff, -jnp.inf))
            t_inv = _unit_lower_inverse(-lower)
            w = _mm_x3(t_inv, (bc * ec) * k)
            u = _mm_x3(t_inv, bc * v)
            uu = u - _mm(w, s)
            o_ref[:, h * B_DV:(h + 1) * B_DV] = ec * _mm(q, s) + _mm(a_qk, uu)
            g_last = gc[last:last + 1, :]
            k_dec = k * jnp.exp(g_last - gc)
            s_ref[j] = jnp.exp(g_last) * s + _mm_tn(k_dec, uu)


def _gdn(qkv, proj, a_log, dt_bias, bsz, seq):
    nchunk = seq // CHUNK
    fwd = lambda b, n: b * nchunk + n
    bwd = lambda b, n: b * nchunk + (nchunk - 1 - n)
    m = bsz * seq
    vec = lambda: pl.BlockSpec((1, LANES), lambda b, n: (0, 0))
    return pl.pallas_call(
        _gdn_kernel,
        grid=(bsz, nchunk),
        in_specs=[pl.BlockSpec((CHUNK, B_QKV), lambda b, n: (fwd(b, n), 0)),
                  pl.BlockSpec((CHUNK, LANES), lambda b, n: (fwd(b, n), OFF["bgt"] // LANES)),
                  pl.BlockSpec((CHUNK, B_QKV), lambda b, n: (bwd(b, n), 0)),
                  pl.BlockSpec((CHUNK, LANES), lambda b, n: (bwd(b, n), OFF["bgt"] // LANES)),
                  vec(), vec()],
        out_specs=[pl.BlockSpec((CHUNK, B_V), lambda b, n: (fwd(b, n), 0)),
                   pl.BlockSpec((CHUNK, B_V), lambda b, n: (bwd(b, n), 0))],
        out_shape=[jax.ShapeDtypeStruct((m, B_V), F32)] * 2,
        scratch_shapes=[pltpu.VMEM((2 * B_HEADS, B_DK, B_DV), F32)],
        compiler_params=_cparams(("parallel", "arbitrary")),
        name="gdn",
    )(qkv, proj, qkv, proj, a_log, dt_bias)


SUB = 16


def _gla_kernel(qf_ref, kf_ref, vf_ref, rf_ref, qb_ref, kb_ref, vb_ref, rb_ref, wg_ref, bg_ref,
                of_ref, ob_ref, s_ref):
    n = pl.program_id(1)

    @pl.when(n == 0)
    def _():
        s_ref[...] = jnp.zeros_like(s_ref)

    qscale = C_DK ** -0.5
    row = lax.broadcasted_iota(jnp.int32, (CHUNK, C_DK), 0)
    for d, (q_ref, k_ref, v_ref, r_ref, o_ref) in enumerate(
            ((qf_ref, kf_ref, vf_ref, rf_ref, of_ref), (qb_ref, kb_ref, vb_ref, rb_ref, ob_ref))):
        rev = d == 1
        incl, _ = _tri_masks(rev)
        cum = jnp.where(incl, 1.0, 0.0).astype(BF16)
        last = 0 if rev else CHUNK - 1
        z = _mm_x3(r_ref[...], wg_ref[:, d * C_QK:(d + 1) * C_QK]) + bg_ref[:, d * C_QK:(d + 1) * C_QK]
        lg = _log_sigmoid(z) / GLA_TAU
        b_all = _mm_exact_lhs(cum, lg)
        for h in range(C_HEADS):
            j = d * C_HEADS + h
            q = q_ref[:, h * C_DK:(h + 1) * C_DK] * qscale
            k = k_ref[:, h * C_DK:(h + 1) * C_DK]
            v = v_ref[:, h * C_DV:(h + 1) * C_DV]
            b = b_all[:, h * C_DK:(h + 1) * C_DK]
            s_t = s_ref[j]
            blocks = []
            for i in range(CHUNK // SUB):
                rows = slice(i * SUB, (i + 1) * SUB)
                ref_row = (i + 1) * SUB - 1 if rev else i * SUB
                b_ref_row = b[ref_row:ref_row + 1, :]
                q_i = q[rows] * jnp.exp(b[rows] - b_ref_row)
                visible = (row >= i * SUB) if rev else (row < (i + 1) * SUB)
                k_i = k * jnp.exp(jnp.where(visible, b_ref_row - b, -jnp.inf))
                blocks.append(_mm_nt(q_i, k_i))
            att = jnp.where(incl, jnp.concatenate(blocks, axis=0), 0.0)
            o_ref[:, h * C_DV:(h + 1) * C_DV] = _mm(att, v) + _mm_nt(q * jnp.exp(b), s_t)
            b_last = b[last:last + 1, :]
            s_ref[j] = jnp.exp(b_last) * s_t + _mm_tn(v, k * jnp.exp(b_last - b))


def _gla(proj, w_gate, b_gate, bsz, seq):
    nchunk = seq // CHUNK
    fwd = lambda b, n: b * nchunk + n
    bwd = lambda b, n: b * nchunk + (nchunk - 1 - n)

    def specs(row):
        return [pl.BlockSpec((CHUNK, C_QK), lambda b, n: (row(b, n), OFF["cq"] // C_QK)),
                pl.BlockSpec((CHUNK, C_QK), lambda b, n: (row(b, n), OFF["ck"] // C_QK)),
                pl.BlockSpec((CHUNK, C_V), lambda b, n: (row(b, n), OFF["cv"] // C_V)),
                pl.BlockSpec((CHUNK, LANES), lambda b, n: (row(b, n), OFF["clr"] // LANES))]

    m = bsz * seq
    return pl.pallas_call(
        _gla_kernel,
        grid=(bsz, nchunk),
        in_specs=specs(fwd) + specs(bwd) + [pl.BlockSpec((LANES, 2 * C_QK), lambda b, n: (0, 0)),
                                            pl.BlockSpec((1, 2 * C_QK), lambda b, n: (0, 0))],
        out_specs=[pl.BlockSpec((CHUNK, C_V), lambda b, n: (fwd(b, n), 0)),
                   pl.BlockSpec((CHUNK, C_V), lambda b, n: (bwd(b, n), 0))],
        out_shape=[jax.ShapeDtypeStruct((m, C_V), F32)] * 2,
        scratch_shapes=[pltpu.VMEM((2 * C_HEADS, C_DV, C_DK), F32)],
        compiler_params=_cparams(("parallel", "arbitrary")),
        name="gla",
    )(proj, proj, proj, proj, proj, proj, proj, proj, w_gate, b_gate)


def _post_kernel(hf_ref, hb_ref, z_ref, g_ref, o_ref, *, head_dim, gate):
    nheads = hf_ref.shape[1] // head_dim
    for h in range(nheads):
        cols = slice(h * head_dim, (h + 1) * head_dim)
        x = hf_ref[:, cols] + hb_ref[:, cols]
        y = x * lax.rsqrt(jnp.mean(x * x, axis=-1, keepdims=True) + EPS) * g_ref[:, cols]
        z = z_ref[:, cols]
        sg = _sigmoid(z)
        o_ref[:, cols] = (y * (sg if gate == "sigmoid" else z * sg)).astype(o_ref.dtype)


def _post(hf, hb, proj, z_off, gain, head_dim, gate, tm=256):
    m, w = hf.shape
    return pl.pallas_call(
        functools.partial(_post_kernel, head_dim=head_dim, gate=gate),
        grid=(m // tm,),
        in_specs=[pl.BlockSpec((tm, w), lambda i: (i, 0)), pl.BlockSpec((tm, w), lambda i: (i, 0)),
                  pl.BlockSpec((tm, w), lambda i: (i, z_off // w)), pl.BlockSpec((1, w), lambda i: (0, 0))],
        out_specs=pl.BlockSpec((tm, w), lambda i: (i, 0)),
        out_shape=jax.ShapeDtypeStruct((m, w), BF16),
        compiler_params=_cparams(("parallel",)),
        name="head_norm_gate",
    )(hf, hb, proj, gain.reshape(1, w))


def _merge_kernel(ya_ref, yb_ref, yc_ref, gh_ref, wb_ref, wg_ref, bg_ref, o_ref):
    gh = gh_ref[...].astype(BF16)
    acc = None
    for j, y_ref in enumerate((ya_ref, yb_ref, yc_ref)):
        gate = _sigmoid(jnp.dot(gh, wg_ref[j], preferred_element_type=F32) + bg_ref[j])
        term = gate * jnp.dot(y_ref[...], wb_ref[j], preferred_element_type=F32)
        acc = term if acc is None else acc + term
    o_ref[...] = acc.astype(o_ref.dtype)


def _merge(ya, yb, yc, proj, w_branch, w_gate, b_gate, tm=512, tn=512):
    m, kb = ya.shape
    n = w_branch.shape[-1]
    ysp = lambda: pl.BlockSpec((tm, kb), lambda i, j: (i, 0))
    return pl.pallas_call(
        _merge_kernel,
        grid=(m // tm, n // tn),
        in_specs=[ysp(), ysp(), ysp(),
                  pl.BlockSpec((tm, GATE_RANK), lambda i, j: (i, OFF["gh"] // GATE_RANK)),
                  pl.BlockSpec((3, kb, tn), lambda i, j: (0, 0, j)),
                  pl.BlockSpec((3, GATE_RANK, tn), lambda i, j: (0, 0, j)),
                  pl.BlockSpec((3, 1, tn), lambda i, j: (0, 0, j))],
        out_specs=pl.BlockSpec((tm, tn), lambda i, j: (i, j)),
        out_shape=jax.ShapeDtypeStruct((m, n), BF16),
        compiler_params=_cparams(("parallel", "parallel")),
        name="merge",
    )(ya, yb, yc, proj, w_branch, w_gate, b_gate)


def _permute_w_in(w_in):
    depth, d, _ = w_in.shape
    parts = []
    for start, w, pad in _PIECES:
        parts.append(w_in[:, :, start:start + w].astype(BF16))
        if pad:
            parts.append(jnp.zeros((depth, d, pad), BF16))
    return jnp.concatenate(parts, axis=-1)


def _pad_lanes(v):
    return jnp.zeros(v.shape[:-1] + (LANES,), F32).at[..., :v.shape[-1]].set(v.astype(F32))


def _gla_gate_weights(w, b):
    depth = w.shape[0]
    wg = jnp.zeros((depth, LANES, 2 * C_QK), F32)
    for n in range(2):
        wg = wg.at[:, n * GLA_RANK:(n + 1) * GLA_RANK, n * C_QK:(n + 1) * C_QK].set(w[:, n].astype(F32))
    return wg, b.astype(F32).reshape(depth, 1, 2 * C_QK)


def _pick_tile(n, prefs):
    for t in prefs:
        if n % t == 0:
            return t
    return n


def kernel(x, norm1_g, w_in, conv_w, conv_b, mlstm_gate_b, mlstm_norm_g, gdn_a_log, gdn_dt_bias, gdn_norm_g, gla_w_gate, gla_b_gate, gla_norm_g, w_branch_a, w_branch_b, w_branch_c, w_merge_gate, b_merge_gate, w_out, norm2_g, w_ff1, w_ff2, final_g):
    bsz, seq, d_model = x.shape
    depth = w_in.shape[0]
    m = bsz * seq
    assert seq % CHUNK == 0 and m % 256 == 0

    w_in_p = _permute_w_in(w_in)
    w_branch = jnp.stack([w_branch_a, w_branch_b, w_branch_c], axis=1).astype(BF16)
    w_mg = w_merge_gate.astype(BF16)
    b_mg = b_merge_gate.astype(F32).reshape(depth, 3, 1, d_model)
    w_out_b = w_out.astype(BF16)
    w_ff1_b = w_ff1.astype(BF16)
    w_ff2_b = w_ff2.astype(BF16)
    gate_bias_a = _pad_lanes(mlstm_gate_b.reshape(depth, 1, 4 * A_HEADS))
    a_log = _pad_lanes(gdn_a_log.reshape(depth, 1, 2 * B_HEADS))
    dt_bias = _pad_lanes(gdn_dt_bias.reshape(depth, 1, 2 * B_HEADS))
    gla_wg, gla_bg = _gla_gate_weights(gla_w_gate, gla_b_gate)

    tm = _pick_tile(m, (1024, 512, 256))
    d_ff = w_ff1.shape[-1]
    xs = x.reshape(m, d_model).astype(F32)
    for l in range(depth):
        xn = _rmsnorm(xs, norm1_g[l], BF16)
        proj = _matmul_fullk(xn, w_in_p[l], tm=tm, tn=_pick_tile(PROJ_W, (640, 128)), out_dtype=F32, name="in_proj")

        ha_f, ha_b = _mlstm(proj, gate_bias_a[l], bsz, seq)
        qkv = _short_conv_qkv(proj, conv_w[l], conv_b[l], bsz, seq)
        hb_f, hb_b = _gdn(qkv, proj, a_log[l], dt_bias[l], bsz, seq)
        hc_f, hc_b = _gla(proj, gla_wg[l], gla_bg[l], bsz, seq)

        ya = _post(ha_f, ha_b, proj, OFF["ao"], mlstm_norm_g[l], A_DV, "sigmoid")
        yb = _post(hb_f, hb_b, proj, OFF["bz"], gdn_norm_g[l], B_DV, "silu")
        yc = _post(hc_f, hc_b, proj, OFF["cr"], gla_norm_g[l], C_DV, "silu")
        mix = _merge(ya, yb, yc, proj, w_branch[l], w_mg[l], b_mg[l], tm=_pick_tile(m, (512, 256)),
                     tn=_pick_tile(d_model, (512, 256, 128)))
        xs = _matmul_fullk(mix, w_out_b[l], tm=tm, tn=_pick_tile(d_model, (512, 256, 128)), out_dtype=F32,
                           epilogue="residual", residual=xs, name="out_proj")

        xn = _rmsnorm(xs, norm2_g[l], BF16)
        hid = _matmul_fullk(xn, w_ff1_b[l], tm=tm, tn=_pick_tile(d_ff, (512, 256, 128)), out_dtype=BF16,
                            epilogue="relu2", name="ff1")
        xs = _matmul_ktiled_residual(hid, w_ff2_b[l], xs, tm=tm, tn=_pick_tile(d_model, (1024, 512, 256, 128)),
                                     tk=_pick_tile(d_ff, (2048, 1024, 512, 256, 128)), name="ff2")
    out = _rmsnorm(xs, final_g, x.dtype)
    return out.reshape(bsz, seq, d_model)
```

```python
import functools

import numpy as np
import jax
import jax.numpy as jnp
from jax import lax
from jax.experimental import pallas as pl
from jax.experimental.pallas import tpu as pltpu

F32 = jnp.float32
BF16 = jnp.bfloat16

A_HEADS, A_DK, A_DV = 4, 128, 256
B_HEADS, B_DK, B_DV = 8, 128, 128
C_HEADS, C_DK, C_DV = 4, 128, 256
GLA_RANK = 16
GLA_TAU = 16.0
GATE_RANK = 256
CONV_K = 5
CHUNK = 64
EPS = 1e-6

A_QK, A_V = A_HEADS * A_DK, A_HEADS * A_DV
B_QK, B_V = B_HEADS * B_DK, B_HEADS * B_DV
B_QKV = 2 * B_QK + B_V
C_QK, C_V = C_HEADS * C_DK, C_HEADS * C_DV
PROJ_SIZES = (A_QK, A_QK, A_V, A_V, 4 * A_HEADS,
              B_QKV, B_V, 4 * B_HEADS,
              C_QK, C_QK, C_V, C_V, 2 * GLA_RANK,
              GATE_RANK)
_SEG_NAMES = ("aq", "ak", "av", "ao", "agt", "bqkv", "bz", "bgt", "cq", "ck", "cv", "cr", "clr", "gh")

LANES = 128
VMEM_LIMIT_BYTES = 56 * 1024 * 1024

_NEW_ORDER = ("bqkv", "bz", "aq", "ak", "av", "ao", "cq", "ck", "cv", "cr", "gh", "agt", "bgt", "clr")
PROJ_TILE = 512


def _layout():
    sizes = dict(zip(_SEG_NAMES, PROJ_SIZES))
    starts = dict(zip(_SEG_NAMES, np.cumsum((0,) + PROJ_SIZES)[:-1].tolist()))
    off, new_off, pieces = 0, {}, []
    for name in _NEW_ORDER:
        new_off[name] = off
        pieces.append((starts[name], sizes[name]))
        off += sizes[name]
    return new_off, pieces, off, off + (-off) % PROJ_TILE


OFF, _PIECES, _PROJ_USED, PROJ_W = _layout()
OFF["gates"] = OFF["agt"]
assert OFF["gates"] % LANES == 0 and _PROJ_USED - OFF["gates"] <= LANES
MLSTM_GATE_LANE = OFF["agt"] - OFF["gates"]
GDN_GATE_LANE = OFF["bgt"] - OFF["gates"]
GLA_GATE_LANE = OFF["clr"] - OFF["gates"]


def _cparams(sem):
    return pltpu.CompilerParams(dimension_semantics=sem, vmem_limit_bytes=VMEM_LIMIT_BYTES)


def _mm(a, b):
    return jnp.dot(a.astype(BF16), b.astype(BF16), preferred_element_type=F32)


def _mm_nt(a, b):
    return lax.dot_general(a.astype(BF16), b.astype(BF16), (((1,), (1,)), ((), ())),
                           preferred_element_type=F32)


def _mm_tn(a, b):
    return lax.dot_general(a.astype(BF16), b.astype(BF16), (((0,), (0,)), ((), ())),
                           preferred_element_type=F32)


def _split3(x):
    hi = x.astype(BF16)
    r = x - hi.astype(F32)
    mid = r.astype(BF16)
    lo = (r - mid.astype(F32)).astype(BF16)
    return hi, mid, lo


def _mm_exact_lhs(t_bf16, x):
    hi, mid, lo = _split3(x)
    d = functools.partial(jnp.dot, preferred_element_type=F32)
    return d(t_bf16, hi) + d(t_bf16, mid) + d(t_bf16, lo)


def _mm_x3(a, b):
    ah = a.astype(BF16)
    al = (a - ah.astype(F32)).astype(BF16)
    bh = b.astype(BF16)
    bl = (b - bh.astype(F32)).astype(BF16)
    d = functools.partial(jnp.dot, preferred_element_type=F32)
    return d(ah, bh) + d(ah, bl) + d(al, bh)


def _softplus(x):
    return jnp.maximum(x, 0.0) + jnp.log1p(jnp.exp(-jnp.abs(x)))


def _log_sigmoid(x):
    return -_softplus(-x)


def _sigmoid(x):
    return 1.0 / (1.0 + jnp.exp(-x))


def _tri_masks(rev):
    r = lax.broadcasted_iota(jnp.int32, (CHUNK, CHUNK), 0)
    c = lax.broadcasted_iota(jnp.int32, (CHUNK, CHUNK), 1)
    incl = (c >= r) if rev else (c <= r)
    strict = (c > r) if rev else (c < r)
    return incl, strict


def _rmsnorm_kernel(x_ref, g_ref, o_ref):
    x = x_ref[...]
    y = x * lax.rsqrt(jnp.mean(x * x, axis=-1, keepdims=True) + EPS)
    o_ref[...] = (y * g_ref[...]).astype(o_ref.dtype)


def _rmsnorm(x, g, out_dtype, tm=256):
    m, d = x.shape
    return pl.pallas_call(
        _rmsnorm_kernel,
        grid=(m // tm,),
        in_specs=[pl.BlockSpec((tm, d), lambda i: (i, 0)), pl.BlockSpec((1, d), lambda i: (0, 0))],
        out_specs=pl.BlockSpec((tm, d), lambda i: (i, 0)),
        out_shape=jax.ShapeDtypeStruct((m, d), out_dtype),
        compiler_params=_cparams(("parallel",)),
        name="rmsnorm",
    )(x, g.reshape(1, d))


def _mm_fullk_kernel(a_ref, b_ref, *rest, epilogue):
    o_ref = rest[-1]
    acc = jnp.dot(a_ref[...], b_ref[...], preferred_element_type=F32)
    if epilogue == "relu2":
        acc = jnp.square(jnp.maximum(acc, 0.0))
    elif epilogue == "residual":
        acc = acc + rest[0][...]
    o_ref[...] = acc.astype(o_ref.dtype)


def _matmul_fullk(a, b, *, tm, tn, out_dtype, epilogue="none", residual=None, name="matmul"):
    m, k = a.shape
    n = b.shape[1]
    in_specs = [pl.BlockSpec((tm, k), lambda i, j: (i, 0)), pl.BlockSpec((k, tn), lambda i, j: (0, j))]
    args = [a, b]
    if epilogue == "residual":
        in_specs.append(pl.BlockSpec((tm, tn), lambda i, j: (i, j)))
        args.append(residual)
    return pl.pallas_call(
        functools.partial(_mm_fullk_kernel, epilogue=epilogue),
        grid=(m // tm, n // tn),
        in_specs=in_specs,
        out_specs=pl.BlockSpec((tm, tn), lambda i, j: (i, j)),
        out_shape=jax.ShapeDtypeStruct((m, n), out_dtype),
        compiler_params=_cparams(("parallel", "parallel")),
        name=name,
    )(*args)


def _mm_ktiled_res_kernel(a_ref, b_ref, r_ref, o_ref, acc_ref):
    k = pl.program_id(2)

    @pl.when(k == 0)
    def _():
        acc_ref[...] = r_ref[...]

    acc_ref[...] += jnp.dot(a_ref[...], b_ref[...], preferred_element_type=F32)

    @pl.when(k == pl.num_programs(2) - 1)
    def _():
        o_ref[...] = acc_ref[...]


def _matmul_ktiled_residual(a, b, residual, *, tm, tn, tk, name="matmul_k"):
    m, kk = a.shape
    n = b.shape[1]
    return pl.pallas_call(
        _mm_ktiled_res_kernel,
        grid=(m // tm, n // tn, kk // tk),
        in_specs=[pl.BlockSpec((tm, tk), lambda i, j, k: (i, k)),
                  pl.BlockSpec((tk, tn), lambda i, j, k: (k, j)),
                  pl.BlockSpec((tm, tn), lambda i, j, k: (i, j))],
        out_specs=pl.BlockSpec((tm, tn), lambda i, j, k: (i, j)),
        out_shape=jax.ShapeDtypeStruct((m, n), F32),
        scratch_shapes=[pltpu.VMEM((tm, tn), F32)],
        compiler_params=_cparams(("parallel", "parallel", "arbitrary")),
        name=name,
    )(a, b, residual)


A_AUG = A_DV + LANES


def _mlstm_kernel(qf_ref, kf_ref, vf_ref, gf_ref, qb_ref, kb_ref, vb_ref, gb_ref, bias_ref,
                  hf_ref, hb_ref, st_ref, m_ref):
    n = pl.program_id(1)

    @pl.when(n == 0)
    def _():
        st_ref[...] = jnp.zeros_like(st_ref)
        m_ref[...] = jnp.zeros_like(m_ref)

    lane = lax.broadcasted_iota(jnp.int32, (CHUNK, LANES), 1)
    ones_tile = jnp.where(lane == 0, 1.0, 0.0).astype(F32)
    kscale = A_DK ** -0.5

    probs = []
    for d, (q_ref, k_ref, v_ref, g_ref, h_ref) in enumerate(
            ((qf_ref, kf_ref, vf_ref, gf_ref, hf_ref), (qb_ref, kb_ref, vb_ref, gb_ref, hb_ref))):
        rev = d == 1
        incl, _ = _tri_masks(rev)
        cum = jnp.where(incl, 1.0, 0.0).astype(BF16)
        g = g_ref[...] + bias_ref[...]
        ls = _log_sigmoid(g)
        f_col = _mm_exact_lhs(cum, ls)
        f_row = f_col.T
        g_row = g.T
        for h in range(A_HEADS):
            ci = MLSTM_GATE_LANE + d * A_HEADS + h
            cf = MLSTM_GATE_LANE + 2 * A_HEADS + d * A_HEADS + h
            j = d * A_HEADS + h
            probs.append(dict(
                j=j, h=h, h_ref=h_ref, incl=incl, last=0 if rev else CHUNK - 1,
                q=q_ref[:, h * A_DK:(h + 1) * A_DK].astype(BF16),
                k=(k_ref[:, h * A_DK:(h + 1) * A_DK] * kscale).astype(BF16),
                v_aug=jnp.concatenate([v_ref[:, h * A_DV:(h + 1) * A_DV], ones_tile], axis=1),
                fc=f_col[:, cf:cf + 1], fr=f_row[cf:cf + 1, :], ir=g_row[ci:ci + 1, :], ic=g[:, ci:ci + 1],
                m_prev=m_ref[j:j + 1, 0:1], st=st_ref[j]))

    for p in probs:
        p["qk"] = _mm_nt(p["q"], p["k"])
        p["q_st"] = _mm(p["q"], p["st"])
    for p in probs:
        d_log = jnp.where(p["incl"], p["fc"] - p["fr"] + p["ir"], -jnp.inf)
        inter = p["fc"] + p["m_prev"]
        p["m_t"] = jnp.maximum(inter, jnp.max(d_log, axis=-1, keepdims=True))
        p["s"] = p["qk"] * jnp.exp(d_log - p["m_t"])
        p["w_inter"] = jnp.exp(inter - p["m_t"])
    for p in probs:
        h = p["h"]
        num = _mm(p["s"], p["v_aug"]) + p["w_inter"] * p["q_st"]
        den = num[:, A_DV:A_DV + 1]
        p["h_ref"][:, h * A_DV:(h + 1) * A_DV] = num[:, :A_DV] / jnp.maximum(jnp.abs(den), jnp.exp(-p["m_t"]))
    for p in probs:
        j, last = p["j"], p["last"]
        m_new = p["m_t"][last:last + 1, :]
        f_last = p["fc"][last:last + 1, :]
        w_k = jnp.exp(f_last - p["fc"] + p["ic"] - m_new)
        a = jnp.exp(f_last + p["m_prev"] - m_new)
        st_ref[j] = a * p["st"] + _mm_tn(p["k"], p["v_aug"] * w_k)
        m_ref[j:j + 1, :] = jnp.broadcast_to(m_new, (1, LANES))


def _mlstm(proj, gate_bias, bsz, seq):
    nchunk = seq // CHUNK
    fwd = lambda b, n: b * nchunk + n
    bwd = lambda b, n: b * nchunk + (nchunk - 1 - n)

    def specs(row):
        return [pl.BlockSpec((CHUNK, A_QK), lambda b, n: (row(b, n), OFF["aq"] // A_QK)),
                pl.BlockSpec((CHUNK, A_QK), lambda b, n: (row(b, n), OFF["ak"] // A_QK)),
                pl.BlockSpec((CHUNK, A_V), lambda b, n: (row(b, n), OFF["av"] // A_V)),
                pl.BlockSpec((CHUNK, LANES), lambda b, n: (row(b, n), OFF["gates"] // LANES))]

    m = bsz * seq
    return pl.pallas_call(
        _mlstm_kernel,
        grid=(bsz, nchunk),
        in_specs=specs(fwd) + specs(bwd) + [pl.BlockSpec((1, LANES), lambda b, n: (0, 0))],
        out_specs=[pl.BlockSpec((CHUNK, A_V), lambda b, n: (fwd(b, n), 0)),
                   pl.BlockSpec((CHUNK, A_V), lambda b, n: (bwd(b, n), 0))],
        out_shape=[jax.ShapeDtypeStruct((m, A_V), F32)] * 2,
        scratch_shapes=[pltpu.VMEM((2 * A_HEADS, A_DK, A_AUG), F32), pltpu.VMEM((2 * A_HEADS, LANES), F32)],
        compiler_params=_cparams(("parallel", "arbitrary")),
        name="mlstm",
    )(proj, proj, proj, proj, proj, proj, proj, proj, gate_bias)


HALO = 8


def _conv_kernel(cur_ref, prev_ref, nxt_ref, w_ref, b_ref, o_ref, buf_ref):
    i = pl.program_id(1)
    tc = cur_ref.shape[0]
    first = i == 0
    last = i == pl.num_programs(1) - 1
    buf_ref[0:HALO, :] = jnp.where(first, 0.0, prev_ref[...])
    buf_ref[HALO:HALO + tc, :] = cur_ref[...]
    buf_ref[HALO + tc:HALO + tc + HALO, :] = jnp.where(last, 0.0, nxt_ref[...])
    pad = CONV_K // 2
    qscale = B_DK ** -0.5
    for c in range(B_QKV // LANES):
        cols = slice(c * LANES, (c + 1) * LANES)
        y = b_ref[:, cols]
        for j in range(CONV_K):
            y = y + w_ref[j:j + 1, cols] * buf_ref[HALO - pad + j:HALO - pad + j + tc, cols]
        y = y * _sigmoid(y)
        if c < 2 * B_HEADS:
            y = y * lax.rsqrt(jnp.sum(y * y, axis=-1, keepdims=True) + EPS)
            if c < B_HEADS:
                y = y * qscale
        o_ref[:, cols] = y


def _short_conv_qkv(proj, conv_w, conv_b, bsz, seq, tc=256):
    nt = seq // tc
    hb = tc // HALO
    nrow8 = bsz * seq // HALO
    cur = lambda b, i: (b * nt + i, 0)
    prev = lambda b, i: (jnp.maximum((b * nt + i) * hb - 1, 0), 0)
    nxt = lambda b, i: (jnp.minimum((b * nt + i + 1) * hb, nrow8 - 1), 0)
    w8 = jnp.zeros((8, B_QKV), F32).at[:CONV_K].set(conv_w)
    return pl.pallas_call(
        _conv_kernel,
        grid=(bsz, nt),
        in_specs=[pl.BlockSpec((tc, B_QKV), cur), pl.BlockSpec((HALO, B_QKV), prev),
                  pl.BlockSpec((HALO, B_QKV), nxt),
                  pl.BlockSpec((8, B_QKV), lambda b, i: (0, 0)), pl.BlockSpec((1, B_QKV), lambda b, i: (0, 0))],
        out_specs=pl.BlockSpec((tc, B_QKV), cur),
        out_shape=jax.ShapeDtypeStruct((bsz * seq, B_QKV), F32),
        scratch_shapes=[pltpu.VMEM((tc + 2 * HALO, B_QKV), F32)],
        compiler_params=_cparams(("parallel", "arbitrary")),
        name="short_conv",
    )(proj, proj, proj, w8, conv_b.reshape(1, B_QKV))


def _gdn_kernel(qkvf_ref, gf_ref, qkvb_ref, gb_ref, alog_ref, dtb_ref, of_ref, ob_ref, s_ref):
    n = pl.program_id(1)

    @pl.when(n == 0)
    def _():
        s_ref[...] = jnp.zeros_like(s_ref)

    r_i = lax.broadcasted_iota(jnp.int32, (CHUNK, CHUNK), 0)
    c_i = lax.broadcasted_iota(jnp.int32, (CHUNK, CHUNK), 1)
    eye = jnp.where(r_i == c_i, 1.0, 0.0).astype(F32)

    probs = []
    for d, (x_ref, g_ref, o_ref) in enumerate(((qkvf_ref, gf_ref, of_ref), (qkvb_ref, gb_ref, ob_ref))):
        rev = d == 1
        incl, strict = _tri_masks(rev)
        cum = jnp.where(incl, 1.0, 0.0).astype(BF16)
        gt = g_ref[...]
        dec = -jnp.exp(alog_ref[...]) * _softplus(gt + dtb_ref[...])
        beta = _sigmoid(gt)
        g_col = _mm_exact_lhs(cum, dec)
        g_row = g_col.T
        e_col = jnp.exp(g_col)
        for h in range(B_HEADS):
            cd = GDN_GATE_LANE + d * B_HEADS + h
            cb = GDN_GATE_LANE + 2 * B_HEADS + d * B_HEADS + h
            probs.append(dict(
                j=d * B_HEADS + h, h=h, o_ref=o_ref, incl=incl, strict=strict, last=0 if rev else CHUNK - 1,
                q=x_ref[:, h * B_DK:(h + 1) * B_DK].astype(BF16),
                k=x_ref[:, B_QK + h * B_DK:B_QK + (h + 1) * B_DK],
                v=x_ref[:, 2 * B_QK + h * B_DV:2 * B_QK + (h + 1) * B_DV],
                gc=g_col[:, cd:cd + 1], gr=g_row[cd:cd + 1, :], ec=e_col[:, cd:cd + 1], bc=beta[:, cb:cb + 1]))

    for p in probs:
        kb = p["k"].astype(BF16)
        p["kq"] = _mm_nt(jnp.concatenate([kb, p["q"]], axis=0), kb)
    for p in probs:
        e = jnp.exp(jnp.where(p["incl"], p["gc"] - p["gr"], -jnp.inf))
        p["a_qk"] = p["kq"][CHUNK:] * e
        p["lmat"] = (p["bc"] * p["kq"][:CHUNK]) * jnp.where(p["strict"], e, 0.0)
    same_block = lambda s: (r_i // s) == (c_i // s)
    for p in probs:
        p["t"] = eye - jnp.where(same_block(2), p["lmat"], 0.0)
    s = 2
    while s < CHUNK:
        couple = same_block(2 * s) & jnp.logical_not(same_block(s))
        for p in probs:
            p["cd"] = _mm(jnp.where(couple, p["lmat"], 0.0), p["t"])
        for p in probs:
            p["t"] = p["t"] - _mm(p["t"], p["cd"])
        s *= 2
    for p in probs:
        rhs = jnp.concatenate([(p["bc"] * p["ec"]) * p["k"], p["bc"] * p["v"]], axis=1)
        p["wu"] = _mm(p["t"], rhs)
    for p in probs:
        p["s"] = s_ref[p["j"]]
        p["wq_s"] = _mm(jnp.concatenate([p["wu"][:, :B_DK].astype(BF16), p["q"]], axis=0), p["s"])
    for p in probs:
        p["uu"] = p["wu"][:, B_DK:] - p["wq_s"][:CHUNK]
    for p in probs:
        h = p["h"]
        p["o_ref"][:, h * B_DV:(h + 1) * B_DV] = p["ec"] * p["wq_s"][CHUNK:] + _mm(p["a_qk"], p["uu"])
    for p in probs:
        g_last = p["gc"][p["last"]:p["last"] + 1, :]
        k_dec = p["k"] * jnp.exp(g_last - p["gc"])
        s_ref[p["j"]] = jnp.exp(g_last) * p["s"] + _mm_tn(k_dec, p["uu"])


def _gdn(qkv, proj, a_log, dt_bias, bsz, seq):
    nchunk = seq // CHUNK
    fwd = lambda b, n: b * nchunk + n
    bwd = lambda b, n: b * nchunk + (nchunk - 1 - n)
    m = bsz * seq
    vec = lambda: pl.BlockSpec((1, LANES), lambda b, n: (0, 0))
    return pl.pallas_call(
        _gdn_kernel,
        grid=(bsz, nchunk),
        in_specs=[pl.BlockSpec((CHUNK, B_QKV), lambda b, n: (fwd(b, n), 0)),
                  pl.BlockSpec((CHUNK, LANES), lambda b, n: (fwd(b, n), OFF["gates"] // LANES)),
                  pl.BlockSpec((CHUNK, B_QKV), lambda b, n: (bwd(b, n), 0)),
                  pl.BlockSpec((CHUNK, LANES), lambda b, n: (bwd(b, n), OFF["gates"] // LANES)),
                  vec(), vec()],
        out_specs=[pl.BlockSpec((CHUNK, B_V), lambda b, n: (fwd(b, n), 0)),
                   pl.BlockSpec((CHUNK, B_V), lambda b, n: (bwd(b, n), 0))],
        out_shape=[jax.ShapeDtypeStruct((m, B_V), F32)] * 2,
        scratch_shapes=[pltpu.VMEM((2 * B_HEADS, B_DK, B_DV), F32)],
        compiler_params=_cparams(("parallel", "arbitrary")),
        name="gdn",
    )(qkv, proj, qkv, proj, a_log, dt_bias)


SUB = 16


def _gla_kernel(qf_ref, kf_ref, vf_ref, rf_ref, qb_ref, kb_ref, vb_ref, rb_ref, wg_ref, bg_ref,
                of_ref, ob_ref, s_ref):
    n = pl.program_id(1)

    @pl.when(n == 0)
    def _():
        s_ref[...] = jnp.zeros_like(s_ref)

    qscale = C_DK ** -0.5
    row = lax.broadcasted_iota(jnp.int32, (CHUNK, C_DK), 0)
    nsub = CHUNK // SUB
    probs = []
    for d, (q_ref, k_ref, v_ref, r_ref, o_ref) in enumerate(
            ((qf_ref, kf_ref, vf_ref, rf_ref, of_ref), (qb_ref, kb_ref, vb_ref, rb_ref, ob_ref))):
        rev = d == 1
        incl, _ = _tri_masks(rev)
        cum = jnp.where(incl, 1.0, 0.0).astype(BF16)
        z = _mm_x3(r_ref[...], wg_ref[:, d * C_QK:(d + 1) * C_QK]) + bg_ref[:, d * C_QK:(d + 1) * C_QK]
        lg = _log_sigmoid(z) / GLA_TAU
        b_all = _mm_exact_lhs(cum, lg)
        for h in range(C_HEADS):
            probs.append(dict(
                j=d * C_HEADS + h, h=h, o_ref=o_ref, rev=rev, incl=incl, last=0 if rev else CHUNK - 1,
                q=q_ref[:, h * C_DK:(h + 1) * C_DK] * qscale, k=k_ref[:, h * C_DK:(h + 1) * C_DK],
                v=v_ref[:, h * C_DV:(h + 1) * C_DV].astype(BF16), b=b_all[:, h * C_DK:(h + 1) * C_DK],
                s_t=s_ref[d * C_HEADS + h]))

    for p in probs:
        q, k, b, rev = p["q"], p["k"], p["b"], p["rev"]
        k_slots, in_blocks = [], []
        b_ref_rows = b
        for i in range(nsub):
            ref_row = (i + 1) * SUB - 1 if rev else i * SUB
            b_ref_row = b[ref_row:ref_row + 1, :]
            in_blocks.append((row >= i * SUB) & (row < (i + 1) * SUB))
            b_ref_rows = jnp.where(in_blocks[i], b_ref_row, b_ref_rows)
            visible = (row >= i * SUB) if rev else (row < (i + 1) * SUB)
            k_slots.append((k * jnp.exp(jnp.where(visible, b_ref_row - b, -jnp.inf))).astype(BF16))
        q_dec = q * jnp.exp(b - b_ref_rows)
        q_slots = [jnp.where(m, q_dec, 0.0).astype(BF16) for m in in_blocks]
        p["att"] = _mm_nt(jnp.concatenate(q_slots, axis=1), jnp.concatenate(k_slots, axis=1))
        p["q_s"] = _mm_nt(q * jnp.exp(b), p["s_t"])
    for p in probs:
        h = p["h"]
        att = jnp.where(p["incl"], p["att"], 0.0)
        p["o_ref"][:, h * C_DV:(h + 1) * C_DV] = _mm(att, p["v"]) + p["q_s"]
    for p in probs:
        b, last = p["b"], p["last"]
        b_last = b[last:last + 1, :]
        s_ref[p["j"]] = jnp.exp(b_last) * p["s_t"] + _mm_tn(p["v"], p["k"] * jnp.exp(b_last - b))


def _gla(proj, w_gate, b_gate, bsz, seq):
    nchunk = seq // CHUNK
    fwd = lambda b, n: b * nchunk + n
    bwd = lambda b, n: b * nchunk + (nchunk - 1 - n)

    def specs(row):
        return [pl.BlockSpec((CHUNK, C_QK), lambda b, n: (row(b, n), OFF["cq"] // C_QK)),
                pl.BlockSpec((CHUNK, C_QK), lambda b, n: (row(b, n), OFF["ck"] // C_QK)),
                pl.BlockSpec((CHUNK, C_V), lambda b, n: (row(b, n), OFF["cv"] // C_V)),
                pl.BlockSpec((CHUNK, LANES), lambda b, n: (row(b, n), OFF["gates"] // LANES))]

    m = bsz * seq
    return pl.pallas_call(
        _gla_kernel,
        grid=(bsz, nchunk),
        in_specs=specs(fwd) + specs(bwd) + [pl.BlockSpec((LANES, 2 * C_QK), lambda b, n: (0, 0)),
                                            pl.BlockSpec((1, 2 * C_QK), lambda b, n: (0, 0))],
        out_specs=[pl.BlockSpec((CHUNK, C_V), lambda b, n: (fwd(b, n), 0)),
                   pl.BlockSpec((CHUNK, C_V), lambda b, n: (bwd(b, n), 0))],
        out_shape=[jax.ShapeDtypeStruct((m, C_V), F32)] * 2,
        scratch_shapes=[pltpu.VMEM((2 * C_HEADS, C_DV, C_DK), F32)],
        compiler_params=_cparams(("parallel", "arbitrary")),
        name="gla",
    )(proj, proj, proj, proj, proj, proj, proj, proj, w_gate, b_gate)


def _post_kernel(hf_ref, hb_ref, z_ref, g_ref, o_ref, *, head_dim, gate):
    nheads = hf_ref.shape[1] // head_dim
    for h in range(nheads):
        cols = slice(h * head_dim, (h + 1) * head_dim)
        x = hf_ref[:, cols] + hb_ref[:, cols]
        y = x * lax.rsqrt(jnp.mean(x * x, axis=-1, keepdims=True) + EPS) * g_ref[:, cols]
        z = z_ref[:, cols]
        sg = _sigmoid(z)
        o_ref[:, cols] = (y * (sg if gate == "sigmoid" else z * sg)).astype(o_ref.dtype)


def _post(hf, hb, proj, z_off, gain, head_dim, gate, tm=256):
    m, w = hf.shape
    return pl.pallas_call(
        functools.partial(_post_kernel, head_dim=head_dim, gate=gate),
        grid=(m // tm,),
        in_specs=[pl.BlockSpec((tm, w), lambda i: (i, 0)), pl.BlockSpec((tm, w), lambda i: (i, 0)),
                  pl.BlockSpec((tm, w), lambda i: (i, z_off // w)), pl.BlockSpec((1, w), lambda i: (0, 0))],
        out_specs=pl.BlockSpec((tm, w), lambda i: (i, 0)),
        out_shape=jax.ShapeDtypeStruct((m, w), BF16),
        compiler_params=_cparams(("parallel",)),
        name="head_norm_gate",
    )(hf, hb, proj, gain.reshape(1, w))


def _merge_kernel(ya_ref, yb_ref, yc_ref, gh_ref, wa_ref, wb_ref, wc_ref, wg_ref, bg_ref, o_ref):
    gh = gh_ref[...].astype(BF16)
    acc = None
    for j, (y_ref, w_ref) in enumerate(((ya_ref, wa_ref), (yb_ref, wb_ref), (yc_ref, wc_ref))):
        gate = _sigmoid(jnp.dot(gh, wg_ref[j], preferred_element_type=F32) + bg_ref[j])
        term = gate * jnp.dot(y_ref[...], w_ref[...], preferred_element_type=F32)
        acc = term if acc is None else acc + term
    o_ref[...] = acc.astype(o_ref.dtype)


def _merge(ya, yb, yc, proj, w_branch, w_gate, b_gate, tm=512, tn=512):
    m, kb = ya.shape
    n = w_gate.shape[-1]
    ysp = lambda: pl.BlockSpec((tm, kb), lambda i, j: (i, 0))
    wsp = lambda: pl.BlockSpec((kb, tn), lambda i, j: (0, j))
    return pl.pallas_call(
        _merge_kernel,
        grid=(m // tm, n // tn),
        in_specs=[ysp(), ysp(), ysp(),
                  pl.BlockSpec((tm, GATE_RANK), lambda i, j: (i, OFF["gh"] // GATE_RANK)),
                  wsp(), wsp(), wsp(),
                  pl.BlockSpec((3, GATE_RANK, tn), lambda i, j: (0, 0, j)),
                  pl.BlockSpec((3, 1, tn), lambda i, j: (0, 0, j))],
        out_specs=pl.BlockSpec((tm, tn), lambda i, j: (i, j)),
        out_shape=jax.ShapeDtypeStruct((m, n), BF16),
        compiler_params=_cparams(("parallel", "parallel")),
        name="merge",
    )(ya, yb, yc, proj, *w_branch, w_gate, b_gate)


def _cast_kernel(x_ref, o_ref):
    o_ref[...] = x_ref[...].astype(o_ref.dtype)


def _cast_bf16(w):
    shape = w.shape
    w2 = w.reshape(-1, shape[-1])
    rows, cols = w2.shape
    tr, tc = _pick_tile(rows, (1024, 512, 256, 128, 64, 32, 16)), _pick_tile(cols, (2048, 1024, 512, 256, 128))
    out = pl.pallas_call(
        _cast_kernel,
        grid=(rows // tr, cols // tc),
        in_specs=[pl.BlockSpec((tr, tc), lambda i, j: (i, j))],
        out_specs=pl.BlockSpec((tr, tc), lambda i, j: (i, j)),
        out_shape=jax.ShapeDtypeStruct((rows, cols), BF16),
        compiler_params=_cparams(("parallel", "parallel")),
        name="cast_bf16",
    )(w2)
    return out.reshape(shape)


_GATE_SEGS = ("agt", "bgt", "clr")


def _permute_w_in_kernel(x_ref, o_ref):
    for (start, w), name in zip(_PIECES, _NEW_ORDER):
        if name not in _GATE_SEGS:
            o_ref[:, OFF[name]:OFF[name] + w] = x_ref[:, start:start + w].astype(BF16)
    lane = lax.broadcasted_iota(jnp.int32, (x_ref.shape[0], LANES), 1)
    tile = jnp.zeros((x_ref.shape[0], LANES), F32)
    for (start, w), name in zip(_PIECES, _NEW_ORDER):
        if name in _GATE_SEGS:
            lane0 = OFF[name] - OFF["gates"]
            src0 = start - lane0
            assert src0 % LANES == 0
            tile = jnp.where((lane >= lane0) & (lane < lane0 + w), x_ref[:, src0:src0 + LANES], tile)
    o_ref[:, OFF["gates"]:OFF["gates"] + LANES] = tile.astype(BF16)
    if PROJ_W > OFF["gates"] + LANES:
        o_ref[:, OFF["gates"] + LANES:] = jnp.zeros((x_ref.shape[0], PROJ_W - OFF["gates"] - LANES), BF16)


def _permute_w_in(w_in):
    depth, d, width = w_in.shape
    rows = depth * d
    tr = _pick_tile(rows, (256, 128, 64, 32, 16))
    out = pl.pallas_call(
        _permute_w_in_kernel,
        grid=(rows // tr,),
        in_specs=[pl.BlockSpec((tr, width), lambda i: (i, 0))],
        out_specs=pl.BlockSpec((tr, PROJ_W), lambda i: (i, 0)),
        out_shape=jax.ShapeDtypeStruct((rows, PROJ_W), BF16),
        compiler_params=_cparams(("parallel",)),
        name="permute_w_in",
    )(w_in.reshape(rows, width))
    return out.reshape(depth, d, PROJ_W)


def _pad_lanes(v, lane0):
    return jnp.zeros(v.shape[:-1] + (LANES,), F32).at[..., lane0:lane0 + v.shape[-1]].set(v.astype(F32))


def _gla_gate_weights(w, b):
    depth = w.shape[0]
    wg = jnp.zeros((depth, LANES, 2 * C_QK), F32)
    for n in range(2):
        r0 = GLA_GATE_LANE + n * GLA_RANK
        wg = wg.at[:, r0:r0 + GLA_RANK, n * C_QK:(n + 1) * C_QK].set(w[:, n].astype(F32))
    return wg, b.astype(F32).reshape(depth, 1, 2 * C_QK)


def _pick_tile(n, prefs):
    for t in prefs:
        if n % t == 0:
            return t
    return n


def kernel(x, norm1_g, w_in, conv_w, conv_b, mlstm_gate_b, mlstm_norm_g, gdn_a_log, gdn_dt_bias, gdn_norm_g, gla_w_gate, gla_b_gate, gla_norm_g, w_branch_a, w_branch_b, w_branch_c, w_merge_gate, b_merge_gate, w_out, norm2_g, w_ff1, w_ff2, final_g):
    bsz, seq, d_model = x.shape
    depth = w_in.shape[0]
    m = bsz * seq
    assert seq % CHUNK == 0 and m % 256 == 0

    w_in_p = _permute_w_in(w_in)
    w_br = [_cast_bf16(w) for w in (w_branch_a, w_branch_b, w_branch_c)]
    w_mg = _cast_bf16(w_merge_gate)
    b_mg = b_merge_gate.astype(F32).reshape(depth, 3, 1, d_model)
    w_out_b = _cast_bf16(w_out)
    w_ff1_b = _cast_bf16(w_ff1)
    w_ff2_b = _cast_bf16(w_ff2)
    gate_bias_a = _pad_lanes(mlstm_gate_b.reshape(depth, 1, 4 * A_HEADS), MLSTM_GATE_LANE)
    a_log = _pad_lanes(gdn_a_log.reshape(depth, 1, 2 * B_HEADS), GDN_GATE_LANE)
    dt_bias = _pad_lanes(gdn_dt_bias.reshape(depth, 1, 2 * B_HEADS), GDN_GATE_LANE)
    gla_wg, gla_bg = _gla_gate_weights(gla_w_gate, gla_b_gate)

    tm = _pick_tile(m, (1024, 512, 256))
    d_ff = w_ff1.shape[-1]
    xs = x.reshape(m, d_model).astype(F32)
    for l in range(depth):
        xn = _rmsnorm(xs, norm1_g[l], BF16)
        proj = _matmul_fullk(xn, w_in_p[l], tm=tm, tn=PROJ_TILE, out_dtype=F32, name="in_proj")

        ha_f, ha_b = _mlstm(proj, gate_bias_a[l], bsz, seq)
        qkv = _short_conv_qkv(proj, conv_w[l], conv_b[l], bsz, seq)
        hb_f, hb_b = _gdn(qkv, proj, a_log[l], dt_bias[l], bsz, seq)
        hc_f, hc_b = _gla(proj, gla_wg[l], gla_bg[l], bsz, seq)

        ya = _post(ha_f, ha_b, proj, OFF["ao"], mlstm_norm_g[l], A_DV, "sigmoid")
        yb = _post(hb_f, hb_b, proj, OFF["bz"], gdn_norm_g[l], B_DV, "silu")
        yc = _post(hc_f, hc_b, proj, OFF["cr"], gla_norm_g[l], C_DV, "silu")
        mix = _merge(ya, yb, yc, proj, [w[l] for w in w_br], w_mg[l], b_mg[l], tm=tm,
                     tn=_pick_tile(d_model, (512, 256, 128)))
        xs = _matmul_fullk(mix, w_out_b[l], tm=tm, tn=_pick_tile(d_model, (512, 256, 128)), out_dtype=F32,
                           epilogue="residual", residual=xs, name="out_proj")

        xn = _rmsnorm(xs, norm2_g[l], BF16)
        hid = _matmul_fullk(xn, w_ff1_b[l], tm=tm, tn=_pick_tile(d_ff, (512, 256, 128)), out_dtype=BF16,
                            epilogue="relu2", name="ff1")
        xs = _matmul_ktiled_residual(hid, w_ff2_b[l], xs, tm=tm, tn=_pick_tile(d_model, (1024, 512, 256, 128)),
                                     tk=_pick_tile(d_ff, (2048, 1024, 512, 256, 128)), name="ff2")
    out = _rmsnorm(xs, final_g, x.dtype)
    return out.reshape(bsz, seq, d_model)
```

```python
import functools
import itertools

import numpy as np
import jax
import jax.numpy as jnp
from jax import lax
from jax.experimental import pallas as pl
from jax.experimental.pallas import tpu as pltpu

F32 = jnp.float32
BF16 = jnp.bfloat16

A_HEADS, A_DK, A_DV = 4, 128, 256
B_HEADS, B_DK, B_DV = 8, 128, 128
C_HEADS, C_DK, C_DV = 4, 128, 256
GLA_RANK = 16
GLA_TAU = 16.0
GATE_RANK = 256
CONV_K = 5
CHUNK = 64
EPS = 1e-6

A_QK, A_V = A_HEADS * A_DK, A_HEADS * A_DV
B_QK, B_V = B_HEADS * B_DK, B_HEADS * B_DV
B_QKV = 2 * B_QK + B_V
C_QK, C_V = C_HEADS * C_DK, C_HEADS * C_DV
PROJ_SIZES = (A_QK, A_QK, A_V, A_V, 4 * A_HEADS,
              B_QKV, B_V, 4 * B_HEADS,
              C_QK, C_QK, C_V, C_V, 2 * GLA_RANK,
              GATE_RANK)
_SEG_NAMES = ("aq", "ak", "av", "ao", "agt", "bqkv", "bz", "bgt", "cq", "ck", "cv", "cr", "clr", "gh")

LANES = 128
VMEM_LIMIT_BYTES = 56 * 1024 * 1024

_NEW_ORDER = ("bqkv", "bz", "aq", "ak", "av", "ao", "cq", "ck", "cv", "cr", "gh", "agt", "bgt", "clr")
PROJ_TILE = 512


def _layout():
    sizes = dict(zip(_SEG_NAMES, PROJ_SIZES))
    starts = dict(zip(_SEG_NAMES, np.cumsum((0,) + PROJ_SIZES)[:-1].tolist()))
    off, new_off, pieces = 0, {}, []
    for name in _NEW_ORDER:
        new_off[name] = off
        pieces.append((starts[name], sizes[name]))
        off += sizes[name]
    return new_off, pieces, off, off + (-off) % PROJ_TILE


OFF, _PIECES, _PROJ_USED, PROJ_W = _layout()
OFF["gates"] = OFF["agt"]
assert OFF["gates"] % LANES == 0 and _PROJ_USED - OFF["gates"] <= LANES
MLSTM_GATE_LANE = OFF["agt"] - OFF["gates"]
GDN_GATE_LANE = OFF["bgt"] - OFF["gates"]
GLA_GATE_LANE = OFF["clr"] - OFF["gates"]


def _cparams(sem):
    return pltpu.CompilerParams(dimension_semantics=sem, vmem_limit_bytes=VMEM_LIMIT_BYTES)


def _mm(a, b):
    return jnp.dot(a.astype(BF16), b.astype(BF16), preferred_element_type=F32)


def _mm_nt(a, b):
    return lax.dot_general(a.astype(BF16), b.astype(BF16), (((1,), (1,)), ((), ())),
                           preferred_element_type=F32)


def _mm_tn(a, b):
    return lax.dot_general(a.astype(BF16), b.astype(BF16), (((0,), (0,)), ((), ())),
                           preferred_element_type=F32)


def _split3(x):
    hi = x.astype(BF16)
    r = x - hi.astype(F32)
    mid = r.astype(BF16)
    lo = (r - mid.astype(F32)).astype(BF16)
    return hi, mid, lo


def _mm_exact_lhs(t_bf16, x):
    hi, mid, lo = _split3(x)
    d = functools.partial(jnp.dot, preferred_element_type=F32)
    return d(t_bf16, hi) + d(t_bf16, mid) + d(t_bf16, lo)


def _mm_x3(a, b):
    ah = a.astype(BF16)
    al = (a - ah.astype(F32)).astype(BF16)
    bh = b.astype(BF16)
    bl = (b - bh.astype(F32)).astype(BF16)
    d = functools.partial(jnp.dot, preferred_element_type=F32)
    return d(ah, bh) + d(ah, bl) + d(al, bh)


def _softplus(x):
    return jnp.maximum(x, 0.0) + jnp.log1p(jnp.exp(-jnp.abs(x)))


def _log_sigmoid(x):
    return -_softplus(-x)


def _sigmoid(x):
    return 1.0 / (1.0 + jnp.exp(-x))


def _tri_masks(rev):
    r = lax.broadcasted_iota(jnp.int32, (CHUNK, CHUNK), 0)
    c = lax.broadcasted_iota(jnp.int32, (CHUNK, CHUNK), 1)
    incl = (c >= r) if rev else (c <= r)
    strict = (c > r) if rev else (c < r)
    return incl, strict


def _rmsnorm_kernel(x_ref, g_ref, o_ref):
    x = x_ref[...]
    y = x * lax.rsqrt(jnp.mean(x * x, axis=-1, keepdims=True) + EPS)
    o_ref[...] = (y * g_ref[...]).astype(o_ref.dtype)


def _rmsnorm(x, g, out_dtype, tm=256):
    m, d = x.shape
    return pl.pallas_call(
        _rmsnorm_kernel,
        grid=(m // tm,),
        in_specs=[pl.BlockSpec((tm, d), lambda i: (i, 0)), pl.BlockSpec((1, d), lambda i: (0, 0))],
        out_specs=pl.BlockSpec((tm, d), lambda i: (i, 0)),
        out_shape=jax.ShapeDtypeStruct((m, d), out_dtype),
        compiler_params=_cparams(("parallel",)),
        name="rmsnorm",
    )(x, g.reshape(1, d))


def _mm_fullk_kernel(a_ref, b_ref, *rest, epilogue):
    o_ref = rest[-1]
    acc = jnp.dot(a_ref[...], b_ref[...], preferred_element_type=F32)
    if epilogue == "relu2":
        acc = jnp.square(jnp.maximum(acc, 0.0))
    elif epilogue == "residual":
        acc = acc + rest[0][...]
    o_ref[...] = acc.astype(o_ref.dtype)


def _matmul_fullk(a, b, layer, *, tm, tn, out_dtype, epilogue="none", residual=None, name="matmul"):
    m, k = a.shape
    n = b.shape[2]
    in_specs = [pl.BlockSpec((tm, k), lambda i, j: (i, 0)), pl.BlockSpec((None, k, tn), lambda i, j: (layer, 0, j))]
    args = [a, b]
    if epilogue == "residual":
        in_specs.append(pl.BlockSpec((tm, tn), lambda i, j: (i, j)))
        args.append(residual)
    return pl.pallas_call(
        functools.partial(_mm_fullk_kernel, epilogue=epilogue),
        grid=(m // tm, n // tn),
        in_specs=in_specs,
        out_specs=pl.BlockSpec((tm, tn), lambda i, j: (i, j)),
        out_shape=jax.ShapeDtypeStruct((m, n), out_dtype),
        compiler_params=_cparams(("parallel", "parallel")),
        name=name,
    )(*args)


def _mm_ktiled_res_kernel(a_ref, b_ref, r_ref, o_ref, acc_ref):
    k = pl.program_id(2)

    @pl.when(k == 0)
    def _():
        acc_ref[...] = r_ref[...]

    acc_ref[...] += jnp.dot(a_ref[...], b_ref[...], preferred_element_type=F32)

    @pl.when(k == pl.num_programs(2) - 1)
    def _():
        o_ref[...] = acc_ref[...]


def _matmul_ktiled_residual(a, b, layer, residual, *, tm, tn, tk, name="matmul_k"):
    m, kk = a.shape
    n = b.shape[2]
    return pl.pallas_call(
        _mm_ktiled_res_kernel,
        grid=(m // tm, n // tn, kk // tk),
        in_specs=[pl.BlockSpec((tm, tk), lambda i, j, k: (i, k)),
                  pl.BlockSpec((None, tk, tn), lambda i, j, k: (layer, k, j)),
                  pl.BlockSpec((tm, tn), lambda i, j, k: (i, j))],
        out_specs=pl.BlockSpec((tm, tn), lambda i, j, k: (i, j)),
        out_shape=jax.ShapeDtypeStruct((m, n), F32),
        scratch_shapes=[pltpu.VMEM((tm, tn), F32)],
        compiler_params=_cparams(("parallel", "parallel", "arbitrary")),
        name=name,
    )(a, b, residual)


A_AUG = A_DV + LANES


def _mlstm_kernel(qf_ref, kf_ref, vf_ref, gf_ref, qb_ref, kb_ref, vb_ref, gb_ref, bias_ref,
                  hf_ref, hb_ref, st_ref, m_ref):
    n = pl.program_id(1)

    @pl.when(n == 0)
    def _():
        st_ref[...] = jnp.zeros_like(st_ref)
        m_ref[...] = jnp.zeros_like(m_ref)

    lane = lax.broadcasted_iota(jnp.int32, (CHUNK, LANES), 1)
    ones_tile = jnp.where(lane == 0, 1.0, 0.0).astype(F32)
    kscale = A_DK ** -0.5

    probs = []
    for d, (q_ref, k_ref, v_ref, g_ref, h_ref) in enumerate(
            ((qf_ref, kf_ref, vf_ref, gf_ref, hf_ref), (qb_ref, kb_ref, vb_ref, gb_ref, hb_ref))):
        rev = d == 1
        incl, _ = _tri_masks(rev)
        cum = jnp.where(incl, 1.0, 0.0).astype(BF16)
        g = g_ref[...] + bias_ref[...]
        ls = _log_sigmoid(g)
        f_col = _mm_exact_lhs(cum, ls)
        f_row = f_col.T
        g_row = g.T
        for h in range(A_HEADS):
            ci = MLSTM_GATE_LANE + d * A_HEADS + h
            cf = MLSTM_GATE_LANE + 2 * A_HEADS + d * A_HEADS + h
            j = d * A_HEADS + h
            probs.append(dict(
                j=j, h=h, h_ref=h_ref, incl=incl, last=0 if rev else CHUNK - 1,
                q=q_ref[:, h * A_DK:(h + 1) * A_DK].astype(BF16),
                k=(k_ref[:, h * A_DK:(h + 1) * A_DK] * kscale).astype(BF16),
                v_aug=jnp.concatenate([v_ref[:, h * A_DV:(h + 1) * A_DV], ones_tile], axis=1),
                fc=f_col[:, cf:cf + 1], fr=f_row[cf:cf + 1, :], ir=g_row[ci:ci + 1, :], ic=g[:, ci:ci + 1],
                m_prev=m_ref[j:j + 1, 0:1], st=st_ref[j]))

    for grp in (probs,):
        for p in grp:
            p["qk"] = _mm_nt(p["q"], p["k"])
            p["q_st"] = _mm(p["q"], p["st"])
        for p in grp:
            d_log = jnp.where(p["incl"], p["fc"] - p["fr"] + p["ir"], -jnp.inf)
            inter = p["fc"] + p["m_prev"]
            p["m_t"] = jnp.maximum(inter, jnp.max(d_log, axis=-1, keepdims=True))
            p["s"] = p["qk"] * jnp.exp(d_log - p["m_t"])
            p["w_inter"] = jnp.exp(inter - p["m_t"])
        for p in grp:
            h = p["h"]
            num = _mm(p["s"], p["v_aug"]) + p["w_inter"] * p["q_st"]
            den = num[:, A_DV:A_DV + 1]
            p["h_ref"][:, h * A_DV:(h + 1) * A_DV] = (
                num[:, :A_DV] / jnp.maximum(jnp.abs(den), jnp.exp(-p["m_t"])))
        for p in grp:
            j, last = p["j"], p["last"]
            m_new = p["m_t"][last:last + 1, :]
            f_last = p["fc"][last:last + 1, :]
            w_k = jnp.exp(f_last - p["fc"] + p["ic"] - m_new)
            a = jnp.exp(f_last + p["m_prev"] - m_new)
            st_ref[j] = a * p["st"] + _mm_tn(p["k"], p["v_aug"] * w_k)
            m_ref[j:j + 1, :] = jnp.broadcast_to(m_new, (1, LANES))


def _chunk_rows(nchunk):
    return (lambda n: n), (lambda n: nchunk - 1 - n)


def _mlstm(proj, gate_bias):
    bsz, seq, _ = proj.shape
    fwd, bwd = _chunk_rows(seq // CHUNK)

    def specs(row):
        return [pl.BlockSpec((None, CHUNK, A_QK), lambda b, n: (b, row(n), OFF["aq"] // A_QK)),
                pl.BlockSpec((None, CHUNK, A_QK), lambda b, n: (b, row(n), OFF["ak"] // A_QK)),
                pl.BlockSpec((None, CHUNK, A_V), lambda b, n: (b, row(n), OFF["av"] // A_V)),
                pl.BlockSpec((None, CHUNK, LANES), lambda b, n: (b, row(n), OFF["gates"] // LANES))]

    nprob = 2 * A_HEADS
    return pl.pallas_call(
        _mlstm_kernel,
        grid=(bsz, seq // CHUNK),
        in_specs=specs(fwd) + specs(bwd) + [pl.BlockSpec((1, LANES), lambda b, n: (0, 0))],
        out_specs=[pl.BlockSpec((None, CHUNK, A_V), lambda b, n: (b, fwd(n), 0)),
                   pl.BlockSpec((None, CHUNK, A_V), lambda b, n: (b, bwd(n), 0))],
        out_shape=[jax.ShapeDtypeStruct((bsz, seq, A_V), F32)] * 2,
        scratch_shapes=[pltpu.VMEM((nprob, A_DK, A_AUG), F32), pltpu.VMEM((nprob, LANES), F32)],
        compiler_params=_cparams(("parallel", "arbitrary")),
        name="mlstm",
    )(proj, proj, proj, proj, proj, proj, proj, proj, gate_bias)


HALO = 8


def _conv_kernel(cur_ref, prev_ref, nxt_ref, w_ref, b_ref, o_ref, buf_ref):
    i = pl.program_id(1)
    tc = cur_ref.shape[0]
    first = i == 0
    last = i == pl.num_programs(1) - 1
    buf_ref[0:HALO, :] = jnp.where(first, 0.0, prev_ref[...])
    buf_ref[HALO:HALO + tc, :] = cur_ref[...]
    buf_ref[HALO + tc:HALO + tc + HALO, :] = jnp.where(last, 0.0, nxt_ref[...])
    pad = CONV_K // 2
    qscale = B_DK ** -0.5
    for c in range(B_QKV // LANES):
        cols = slice(c * LANES, (c + 1) * LANES)
        y = b_ref[:, cols]
        for j in range(CONV_K):
            y = y + w_ref[j:j + 1, cols] * buf_ref[HALO - pad + j:HALO - pad + j + tc, cols]
        y = y * _sigmoid(y)
        if c < 2 * B_HEADS:
            y = y * lax.rsqrt(jnp.sum(y * y, axis=-1, keepdims=True) + EPS)
            if c < B_HEADS:
                y = y * qscale
        o_ref[:, cols] = y


def _short_conv_qkv(proj, conv_w, conv_b, bsz, seq, tc=256):
    nt = seq // tc
    hb = tc // HALO
    nrow8 = bsz * seq // HALO
    cur = lambda b, i: (b * nt + i, 0)
    prev = lambda b, i: (jnp.maximum((b * nt + i) * hb - 1, 0), 0)
    nxt = lambda b, i: (jnp.minimum((b * nt + i + 1) * hb, nrow8 - 1), 0)
    w8 = jnp.zeros((8, B_QKV), F32).at[:CONV_K].set(conv_w)
    return pl.pallas_call(
        _conv_kernel,
        grid=(bsz, nt),
        in_specs=[pl.BlockSpec((tc, B_QKV), cur), pl.BlockSpec((HALO, B_QKV), prev),
                  pl.BlockSpec((HALO, B_QKV), nxt),
                  pl.BlockSpec((8, B_QKV), lambda b, i: (0, 0)), pl.BlockSpec((1, B_QKV), lambda b, i: (0, 0))],
        out_specs=pl.BlockSpec((tc, B_QKV), cur),
        out_shape=jax.ShapeDtypeStruct((bsz * seq, B_QKV), F32),
        scratch_shapes=[pltpu.VMEM((tc + 2 * HALO, B_QKV), F32)],
        compiler_params=_cparams(("parallel", "arbitrary")),
        name="short_conv",
    )(proj, proj, proj, w8, conv_b.reshape(1, B_QKV))


def _gdn_kernel(qkvf_ref, gf_ref, qkvb_ref, gb_ref, alog_ref, dtb_ref, of_ref, ob_ref, s_ref):
    n = pl.program_id(0)
    bsz = qkvf_ref.shape[0]

    @pl.when(n == 0)
    def _():
        s_ref[...] = jnp.zeros_like(s_ref)

    r_i = lax.broadcasted_iota(jnp.int32, (CHUNK, CHUNK), 0)
    c_i = lax.broadcasted_iota(jnp.int32, (CHUNK, CHUNK), 1)
    eye = jnp.where(r_i == c_i, 1.0, 0.0).astype(F32)

    probs = []
    for b, (d, (x_ref, g_ref, o_ref)) in itertools.product(
            range(bsz), enumerate(((qkvf_ref, gf_ref, of_ref), (qkvb_ref, gb_ref, ob_ref)))):
        rev = d == 1
        incl, strict = _tri_masks(rev)
        cum = jnp.where(incl, 1.0, 0.0).astype(BF16)
        gt = g_ref[b]
        dec = -jnp.exp(alog_ref[...]) * _softplus(gt + dtb_ref[...])
        beta = _sigmoid(gt)
        g_col = _mm_exact_lhs(cum, dec)
        g_row = g_col.T
        e_col = jnp.exp(g_col)
        for h in range(B_HEADS):
            cd = GDN_GATE_LANE + d * B_HEADS + h
            cb = GDN_GATE_LANE + 2 * B_HEADS + d * B_HEADS + h
            probs.append(dict(
                j=(b * 2 + d) * B_HEADS + h, b=b, h=h, o_ref=o_ref, incl=incl, strict=strict,
                last=0 if rev else CHUNK - 1,
                q=x_ref[b, :, h * B_DK:(h + 1) * B_DK].astype(BF16),
                k=x_ref[b, :, B_QK + h * B_DK:B_QK + (h + 1) * B_DK],
                v=x_ref[b, :, 2 * B_QK + h * B_DV:2 * B_QK + (h + 1) * B_DV],
                gc=g_col[:, cd:cd + 1], gr=g_row[cd:cd + 1, :], ec=e_col[:, cd:cd + 1], bc=beta[:, cb:cb + 1]))

    for p in probs:
        kb = p["k"].astype(BF16)
        p["kq"] = _mm_nt(jnp.concatenate([kb, p["q"]], axis=0), kb)
    for p in probs:
        e = jnp.exp(jnp.where(p["incl"], p["gc"] - p["gr"], -jnp.inf))
        p["a_qk"] = p["kq"][CHUNK:] * e
        p["lmat"] = (p["bc"] * p["kq"][:CHUNK]) * jnp.where(p["strict"], e, 0.0)
    same_block = lambda s: (r_i // s) == (c_i // s)
    for p in probs:
        p["t"] = eye - jnp.where(same_block(2), p["lmat"], 0.0)
    s = 2
    while s < CHUNK:
        couple = same_block(2 * s) & jnp.logical_not(same_block(s))
        for p in probs:
            p["cd"] = _mm(jnp.where(couple, p["lmat"], 0.0), p["t"])
        for p in probs:
            p["t"] = p["t"] - _mm(p["t"], p["cd"])
        s *= 2
    for p in probs:
        rhs = jnp.concatenate([(p["bc"] * p["ec"]) * p["k"], p["bc"] * p["v"]], axis=1)
        p["wu"] = _mm(p["t"], rhs)
    for p in probs:
        p["s"] = s_ref[p["j"]]
        p["wq_s"] = _mm(jnp.concatenate([p["wu"][:, :B_DK].astype(BF16), p["q"]], axis=0), p["s"])
    for p in probs:
        p["uu"] = p["wu"][:, B_DK:] - p["wq_s"][:CHUNK]
    for p in probs:
        h = p["h"]
        p["o_ref"][p["b"], :, h * B_DV:(h + 1) * B_DV] = p["ec"] * p["wq_s"][CHUNK:] + _mm(p["a_qk"], p["uu"])
    for p in probs:
        g_last = p["gc"][p["last"]:p["last"] + 1, :]
        k_dec = p["k"] * jnp.exp(g_last - p["gc"])
        s_ref[p["j"]] = jnp.exp(g_last) * p["s"] + _mm_tn(k_dec, p["uu"])


def _gdn(qkv, proj, a_log, dt_bias):
    bsz, seq, _ = proj.shape
    fwd, bwd = _chunk_rows(seq // CHUNK)
    vec = lambda: pl.BlockSpec((1, LANES), lambda n: (0, 0))
    return pl.pallas_call(
        _gdn_kernel,
        grid=(seq // CHUNK,),
        in_specs=[pl.BlockSpec((bsz, CHUNK, B_QKV), lambda n: (0, fwd(n), 0)),
                  pl.BlockSpec((bsz, CHUNK, LANES), lambda n: (0, fwd(n), OFF["gates"] // LANES)),
                  pl.BlockSpec((bsz, CHUNK, B_QKV), lambda n: (0, bwd(n), 0)),
                  pl.BlockSpec((bsz, CHUNK, LANES), lambda n: (0, bwd(n), OFF["gates"] // LANES)),
                  vec(), vec()],
        out_specs=[pl.BlockSpec((bsz, CHUNK, B_V), lambda n: (0, fwd(n), 0)),
                   pl.BlockSpec((bsz, CHUNK, B_V), lambda n: (0, bwd(n), 0))],
        out_shape=[jax.ShapeDtypeStruct((bsz, seq, B_V), F32)] * 2,
        scratch_shapes=[pltpu.VMEM((bsz * 2 * B_HEADS, B_DK, B_DV), F32)],
        compiler_params=_cparams(("arbitrary",)),
        name="gdn",
    )(qkv, proj, qkv, proj, a_log, dt_bias)


SUB = 16


def _gla_kernel(qf_ref, kf_ref, vf_ref, rf_ref, qb_ref, kb_ref, vb_ref, rb_ref, wg_ref, bg_ref,
                of_ref, ob_ref, s_ref):
    n = pl.program_id(0)
    bsz = qf_ref.shape[0]

    @pl.when(n == 0)
    def _():
        s_ref[...] = jnp.zeros_like(s_ref)

    qscale = C_DK ** -0.5
    row = lax.broadcasted_iota(jnp.int32, (CHUNK, C_DK), 0)
    nsub = CHUNK // SUB
    probs = []
    for b, (d, (q_ref, k_ref, v_ref, r_ref, o_ref)) in itertools.product(range(bsz), enumerate(
            ((qf_ref, kf_ref, vf_ref, rf_ref, of_ref), (qb_ref, kb_ref, vb_ref, rb_ref, ob_ref)))):
        rev = d == 1
        incl, _ = _tri_masks(rev)
        cum = jnp.where(incl, 1.0, 0.0).astype(BF16)
        z = _mm_x3(r_ref[b], wg_ref[:, d * C_QK:(d + 1) * C_QK]) + bg_ref[:, d * C_QK:(d + 1) * C_QK]
        lg = _log_sigmoid(z) / GLA_TAU
        b_all = _mm_exact_lhs(cum, lg)
        for h in range(C_HEADS):
            j = (b * 2 + d) * C_HEADS + h
            probs.append(dict(
                j=j, bi=b, h=h, o_ref=o_ref, rev=rev, incl=incl, last=0 if rev else CHUNK - 1,
                q=q_ref[b, :, h * C_DK:(h + 1) * C_DK] * qscale, k=k_ref[b, :, h * C_DK:(h + 1) * C_DK],
                v=v_ref[b, :, h * C_DV:(h + 1) * C_DV].astype(BF16), b=b_all[:, h * C_DK:(h + 1) * C_DK],
                s_t=s_ref[j]))

    for p in probs:
        q, k, b, rev = p["q"], p["k"], p["b"], p["rev"]
        k_slots, in_blocks = [], []
        b_ref_rows = b
        for i in range(nsub):
            ref_row = (i + 1) * SUB - 1 if rev else i * SUB
            b_ref_row = b[ref_row:ref_row + 1, :]
            in_blocks.append((row >= i * SUB) & (row < (i + 1) * SUB))
            b_ref_rows = jnp.where(in_blocks[i], b_ref_row, b_ref_rows)
            visible = (row >= i * SUB) if rev else (row < (i + 1) * SUB)
            k_slots.append((k * jnp.exp(jnp.where(visible, b_ref_row - b, -jnp.inf))).astype(BF16))
        q_dec = q * jnp.exp(b - b_ref_rows)
        q_slots = [jnp.where(m, q_dec, 0.0).astype(BF16) for m in in_blocks]
        p["att"] = _mm_nt(jnp.concatenate(q_slots, axis=1), jnp.concatenate(k_slots, axis=1))
        p["q_s"] = _mm_nt(q * jnp.exp(b), p["s_t"])
    for p in probs:
        h = p["h"]
        att = jnp.where(p["incl"], p["att"], 0.0)
        p["o_ref"][p["bi"], :, h * C_DV:(h + 1) * C_DV] = _mm(att, p["v"]) + p["q_s"]
    for p in probs:
        b, last = p["b"], p["last"]
        b_last = b[last:last + 1, :]
        s_ref[p["j"]] = jnp.exp(b_last) * p["s_t"] + _mm_tn(p["v"], p["k"] * jnp.exp(b_last - b))


def _gla(proj, w_gate, b_gate):
    bsz, seq, _ = proj.shape
    fwd, bwd = _chunk_rows(seq // CHUNK)

    def specs(row):
        return [pl.BlockSpec((bsz, CHUNK, C_QK), lambda n: (0, row(n), OFF["cq"] // C_QK)),
                pl.BlockSpec((bsz, CHUNK, C_QK), lambda n: (0, row(n), OFF["ck"] // C_QK)),
                pl.BlockSpec((bsz, CHUNK, C_V), lambda n: (0, row(n), OFF["cv"] // C_V)),
                pl.BlockSpec((bsz, CHUNK, LANES), lambda n: (0, row(n), OFF["gates"] // LANES))]

    return pl.pallas_call(
        _gla_kernel,
        grid=(seq // CHUNK,),
        in_specs=specs(fwd) + specs(bwd) + [pl.BlockSpec((LANES, 2 * C_QK), lambda n: (0, 0)),
                                            pl.BlockSpec((1, 2 * C_QK), lambda n: (0, 0))],
        out_specs=[pl.BlockSpec((bsz, CHUNK, C_V), lambda n: (0, fwd(n), 0)),
                   pl.BlockSpec((bsz, CHUNK, C_V), lambda n: (0, bwd(n), 0))],
        out_shape=[jax.ShapeDtypeStruct((bsz, seq, C_V), F32)] * 2,
        scratch_shapes=[pltpu.VMEM((bsz * 2 * C_HEADS, C_DV, C_DK), F32)],
        compiler_params=_cparams(("arbitrary",)),
        name="gla",
    )(proj, proj, proj, proj, proj, proj, proj, proj, w_gate, b_gate)


def _post_kernel(hf_ref, hb_ref, z_ref, g_ref, o_ref, *, head_dim, gate):
    nheads = hf_ref.shape[1] // head_dim
    for h in range(nheads):
        cols = slice(h * head_dim, (h + 1) * head_dim)
        x = hf_ref[:, cols] + hb_ref[:, cols]
        y = x * lax.rsqrt(jnp.mean(x * x, axis=-1, keepdims=True) + EPS) * g_ref[:, cols]
        z = z_ref[:, cols]
        sg = _sigmoid(z)
        o_ref[:, cols] = (y * (sg if gate == "sigmoid" else z * sg)).astype(o_ref.dtype)


def _post(hf, hb, proj, z_off, gain, head_dim, gate, tm=256):
    m, w = hf.shape
    return pl.pallas_call(
        functools.partial(_post_kernel, head_dim=head_dim, gate=gate),
        grid=(m // tm,),
        in_specs=[pl.BlockSpec((tm, w), lambda i: (i, 0)), pl.BlockSpec((tm, w), lambda i: (i, 0)),
                  pl.BlockSpec((tm, w), lambda i: (i, z_off // w)), pl.BlockSpec((1, w), lambda i: (0, 0))],
        out_specs=pl.BlockSpec((tm, w), lambda i: (i, 0)),
        out_shape=jax.ShapeDtypeStruct((m, w), BF16),
        compiler_params=_cparams(("parallel",)),
        name="head_norm_gate",
    )(hf, hb, proj, gain.reshape(1, w))


def _merge_kernel(ya_ref, yb_ref, yc_ref, gh_ref, wa_ref, wb_ref, wc_ref, wg_ref, bg_ref, o_ref):
    gh = gh_ref[...].astype(BF16)
    acc = None
    for j, (y_ref, w_ref) in enumerate(((ya_ref, wa_ref), (yb_ref, wb_ref), (yc_ref, wc_ref))):
        gate = _sigmoid(jnp.dot(gh, wg_ref[j], preferred_element_type=F32) + bg_ref[j])
        term = gate * jnp.dot(y_ref[...], w_ref[...], preferred_element_type=F32)
        acc = term if acc is None else acc + term
    o_ref[...] = acc.astype(o_ref.dtype)


def _merge(ya, yb, yc, proj, w_branch, w_gate, b_gate, layer, tm=512, tn=512):
    m, kb = ya.shape
    n = w_gate.shape[-1]
    ysp = lambda: pl.BlockSpec((tm, kb), lambda i, j: (i, 0))
    wsp = lambda: pl.BlockSpec((None, kb, tn), lambda i, j: (layer, 0, j))
    return pl.pallas_call(
        _merge_kernel,
        grid=(m // tm, n // tn),
        in_specs=[ysp(), ysp(), ysp(),
                  pl.BlockSpec((tm, GATE_RANK), lambda i, j: (i, OFF["gh"] // GATE_RANK)),
                  wsp(), wsp(), wsp(),
                  pl.BlockSpec((None, 3, GATE_RANK, tn), lambda i, j: (layer, 0, 0, j)),
                  pl.BlockSpec((None, 3, 1, tn), lambda i, j: (layer, 0, 0, j))],
        out_specs=pl.BlockSpec((tm, tn), lambda i, j: (i, j)),
        out_shape=jax.ShapeDtypeStruct((m, n), BF16),
        compiler_params=_cparams(("parallel", "parallel")),
        name="merge",
    )(ya, yb, yc, proj, *w_branch, w_gate, b_gate)


def _cast_kernel(x_ref, o_ref):
    o_ref[...] = x_ref[...].astype(o_ref.dtype)


def _cast_bf16(w):
    shape = w.shape
    lead, (rows, cols) = shape[:-2], shape[-2:]
    tr, tc = _pick_tile(rows, (1024, 512, 256, 128, 64, 32, 16)), _pick_tile(cols, (2048, 1024, 512, 256, 128))
    nl = len(lead)
    spec = pl.BlockSpec((None,) * nl + (tr, tc), lambda *g: g)
    return pl.pallas_call(
        _cast_kernel,
        grid=lead + (rows // tr, cols // tc),
        in_specs=[spec],
        out_specs=spec,
        out_shape=jax.ShapeDtypeStruct(shape, BF16),
        compiler_params=_cparams(("parallel",) * (nl + 2)),
        name="cast_bf16",
    )(w)


_GATE_SEGS = ("agt", "bgt", "clr")


def _permute_w_in_kernel(x_ref, o_ref):
    for (start, w), name in zip(_PIECES, _NEW_ORDER):
        if name not in _GATE_SEGS:
            o_ref[:, OFF[name]:OFF[name] + w] = x_ref[:, start:start + w].astype(BF16)
    lane = lax.broadcasted_iota(jnp.int32, (x_ref.shape[0], LANES), 1)
    tile = jnp.zeros((x_ref.shape[0], LANES), F32)
    for (start, w), name in zip(_PIECES, _NEW_ORDER):
        if name in _GATE_SEGS:
            lane0 = OFF[name] - OFF["gates"]
            src0 = start - lane0
            assert src0 % LANES == 0
            tile = jnp.where((lane >= lane0) & (lane < lane0 + w), x_ref[:, src0:src0 + LANES], tile)
    o_ref[:, OFF["gates"]:OFF["gates"] + LANES] = tile.astype(BF16)
    if PROJ_W > OFF["gates"] + LANES:
        o_ref[:, OFF["gates"] + LANES:] = jnp.zeros((x_ref.shape[0], PROJ_W - OFF["gates"] - LANES), BF16)


def _permute_w_in(w_in):
    depth, d, width = w_in.shape
    tr = _pick_tile(d, (256, 128, 64, 32, 16))
    return pl.pallas_call(
        _permute_w_in_kernel,
        grid=(depth, d // tr),
        in_specs=[pl.BlockSpec((None, tr, width), lambda l, i: (l, i, 0))],
        out_specs=pl.BlockSpec((None, tr, PROJ_W), lambda l, i: (l, i, 0)),
        out_shape=jax.ShapeDtypeStruct((depth, d, PROJ_W), BF16),
        compiler_params=_cparams(("parallel", "parallel")),
        name="permute_w_in",
    )(w_in)


def _pad_lanes(v, lane0):
    return jnp.zeros(v.shape[:-1] + (LANES,), F32).at[..., lane0:lane0 + v.shape[-1]].set(v.astype(F32))


def _gla_gate_weights(w, b):
    depth = w.shape[0]
    wg = jnp.zeros((depth, LANES, 2 * C_QK), F32)
    for n in range(2):
        r0 = GLA_GATE_LANE + n * GLA_RANK
        wg = wg.at[:, r0:r0 + GLA_RANK, n * C_QK:(n + 1) * C_QK].set(w[:, n].astype(F32))
    return wg, b.astype(F32).reshape(depth, 1, 2 * C_QK)


def _pick_tile(n, prefs):
    for t in prefs:
        if n % t == 0:
            return t
    return n


def kernel(x, norm1_g, w_in, conv_w, conv_b, mlstm_gate_b, mlstm_norm_g, gdn_a_log, gdn_dt_bias, gdn_norm_g, gla_w_gate, gla_b_gate, gla_norm_g, w_branch_a, w_branch_b, w_branch_c, w_merge_gate, b_merge_gate, w_out, norm2_g, w_ff1, w_ff2, final_g):
    bsz, seq, d_model = x.shape
    depth = w_in.shape[0]
    m = bsz * seq
    assert seq % CHUNK == 0 and m % 256 == 0

    w_in_p = _permute_w_in(w_in)
    w_br = [_cast_bf16(w) for w in (w_branch_a, w_branch_b, w_branch_c)]
    w_mg = _cast_bf16(w_merge_gate)
    b_mg = b_merge_gate.astype(F32).reshape(depth, 3, 1, d_model)
    w_out_b = _cast_bf16(w_out)
    w_ff1_b = _cast_bf16(w_ff1)
    w_ff2_b = _cast_bf16(w_ff2)
    gate_bias_a = _pad_lanes(mlstm_gate_b.reshape(depth, 1, 4 * A_HEADS), MLSTM_GATE_LANE)
    a_log = _pad_lanes(gdn_a_log.reshape(depth, 1, 2 * B_HEADS), GDN_GATE_LANE)
    dt_bias = _pad_lanes(gdn_dt_bias.reshape(depth, 1, 2 * B_HEADS), GDN_GATE_LANE)
    gla_wg, gla_bg = _gla_gate_weights(gla_w_gate, gla_b_gate)

    tm = _pick_tile(m, (1024, 512, 256))
    d_ff = w_ff1.shape[-1]
    xs = x.reshape(m, d_model).astype(F32)
    for l in range(depth):
        xn = _rmsnorm(xs, norm1_g[l], BF16)
        proj = _matmul_fullk(xn, w_in_p, l, tm=tm, tn=PROJ_TILE, out_dtype=F32, name="in_proj")

        proj3 = proj.reshape(bsz, seq, PROJ_W)
        flat = lambda hs: [h.reshape(m, h.shape[-1]) for h in hs]
        ha_f, ha_b = flat(_mlstm(proj3, gate_bias_a[l]))
        qkv = _short_conv_qkv(proj, conv_w[l], conv_b[l], bsz, seq)
        hb_f, hb_b = flat(_gdn(qkv.reshape(bsz, seq, B_QKV), proj3, a_log[l], dt_bias[l]))
        hc_f, hc_b = flat(_gla(proj3, gla_wg[l], gla_bg[l]))

        ya = _post(ha_f, ha_b, proj, OFF["ao"], mlstm_norm_g[l], A_DV, "sigmoid")
        yb = _post(hb_f, hb_b, proj, OFF["bz"], gdn_norm_g[l], B_DV, "silu")
        yc = _post(hc_f, hc_b, proj, OFF["cr"], gla_norm_g[l], C_DV, "silu")
        mix = _merge(ya, yb, yc, proj, w_br, w_mg, b_mg, l, tm=tm, tn=_pick_tile(d_model, (512, 256, 128)))
        xs = _matmul_fullk(mix, w_out_b, l, tm=tm, tn=_pick_tile(d_model, (512, 256, 128)), out_dtype=F32,
                           epilogue="residual", residual=xs, name="out_proj")

        xn = _rmsnorm(xs, norm2_g[l], BF16)
        hid = _matmul_fullk(xn, w_ff1_b, l, tm=tm, tn=_pick_tile(d_ff, (1024, 512, 256, 128)), out_dtype=BF16,
                            epilogue="relu2", name="ff1")
        xs = _matmul_ktiled_residual(hid, w_ff2_b, l, xs, tm=tm, tn=_pick_tile(d_model, (1024, 512, 256, 128)),
                                     tk=_pick_tile(d_ff, (2048, 1024, 512, 256, 128)), name="ff2")
    out = _rmsnorm(xs, final_g, x.dtype)
    return out.reshape(bsz, seq, d_model)
```

```python
import functools
import itertools

import numpy as np
import jax
import jax.numpy as jnp
from jax import lax
from jax.experimental import pallas as pl
from jax.experimental.pallas import tpu as pltpu

F32 = jnp.float32
BF16 = jnp.bfloat16

A_HEADS, A_DK, A_DV = 4, 128, 256
B_HEADS, B_DK, B_DV = 8, 128, 128
C_HEADS, C_DK, C_DV = 4, 128, 256
GLA_RANK = 16
GLA_TAU = 16.0
GATE_RANK = 256
CONV_K = 5
CHUNK = 64
EPS = 1e-6

A_QK, A_V = A_HEADS * A_DK, A_HEADS * A_DV
B_QK, B_V = B_HEADS * B_DK, B_HEADS * B_DV
B_QKV = 2 * B_QK + B_V
C_QK, C_V = C_HEADS * C_DK, C_HEADS * C_DV
PROJ_SIZES = (A_QK, A_QK, A_V, A_V, 4 * A_HEADS,
              B_QKV, B_V, 4 * B_HEADS,
              C_QK, C_QK, C_V, C_V, 2 * GLA_RANK,
              GATE_RANK)
_SEG_NAMES = ("aq", "ak", "av", "ao", "agt", "bqkv", "bz", "bgt", "cq", "ck", "cv", "cr", "clr", "gh")

LANES = 128
VMEM_LIMIT_BYTES = 56 * 1024 * 1024

_NEW_ORDER = ("bqkv", "bz", "aq", "ak", "av", "ao", "cq", "ck", "cv", "cr", "gh", "agt", "bgt", "clr")
PROJ_TILE = 512


def _layout():
    sizes = dict(zip(_SEG_NAMES, PROJ_SIZES))
    starts = dict(zip(_SEG_NAMES, np.cumsum((0,) + PROJ_SIZES)[:-1].tolist()))
    off, new_off, pieces = 0, {}, []
    for name in _NEW_ORDER:
        new_off[name] = off
        pieces.append((starts[name], sizes[name]))
        off += sizes[name]
    return new_off, pieces, off, off + (-off) % PROJ_TILE


OFF, _PIECES, _PROJ_USED, PROJ_W = _layout()
OFF["gates"] = OFF["agt"]
assert OFF["gates"] % LANES == 0 and _PROJ_USED - OFF["gates"] <= LANES
MLSTM_GATE_LANE = OFF["agt"] - OFF["gates"]
GDN_GATE_LANE = OFF["bgt"] - OFF["gates"]
GLA_GATE_LANE = OFF["clr"] - OFF["gates"]


def _cparams(sem):
    return pltpu.CompilerParams(dimension_semantics=sem, vmem_limit_bytes=VMEM_LIMIT_BYTES)


def _mm(a, b):
    return jnp.dot(a.astype(BF16), b.astype(BF16), preferred_element_type=F32)


def _mm_nt(a, b):
    return lax.dot_general(a.astype(BF16), b.astype(BF16), (((1,), (1,)), ((), ())),
                           preferred_element_type=F32)


def _mm_tn(a, b):
    return lax.dot_general(a.astype(BF16), b.astype(BF16), (((0,), (0,)), ((), ())),
                           preferred_element_type=F32)


def _split3(x):
    hi = x.astype(BF16)
    r = x - hi.astype(F32)
    mid = r.astype(BF16)
    lo = (r - mid.astype(F32)).astype(BF16)
    return hi, mid, lo


def _mm_exact_lhs(t_bf16, x):
    hi, mid, lo = _split3(x)
    d = functools.partial(jnp.dot, preferred_element_type=F32)
    return d(t_bf16, hi) + d(t_bf16, mid) + d(t_bf16, lo)


def _mm_x3(a, b):
    ah = a.astype(BF16)
    al = (a - ah.astype(F32)).astype(BF16)
    bh = b.astype(BF16)
    bl = (b - bh.astype(F32)).astype(BF16)
    d = functools.partial(jnp.dot, preferred_element_type=F32)
    return d(ah, bh) + d(ah, bl) + d(al, bh)


def _softplus(x):
    return jnp.maximum(x, 0.0) + jnp.log1p(jnp.exp(-jnp.abs(x)))


def _log_sigmoid(x):
    return -_softplus(-x)


def _sigmoid(x):
    return 1.0 / (1.0 + jnp.exp(-x))


def _tri_masks(rev):
    r = lax.broadcasted_iota(jnp.int32, (CHUNK, CHUNK), 0)
    c = lax.broadcasted_iota(jnp.int32, (CHUNK, CHUNK), 1)
    incl = (c >= r) if rev else (c <= r)
    strict = (c > r) if rev else (c < r)
    return incl, strict


def _rmsnorm_kernel(x_ref, g_ref, o_ref):
    x = x_ref[...]
    y = x * lax.rsqrt(jnp.mean(x * x, axis=-1, keepdims=True) + EPS)
    o_ref[...] = (y * g_ref[...]).astype(o_ref.dtype)


def _rmsnorm(x, g, out_dtype, tm=256):
    m, d = x.shape
    return pl.pallas_call(
        _rmsnorm_kernel,
        grid=(m // tm,),
        in_specs=[pl.BlockSpec((tm, d), lambda i: (i, 0)), pl.BlockSpec((1, d), lambda i: (0, 0))],
        out_specs=pl.BlockSpec((tm, d), lambda i: (i, 0)),
        out_shape=jax.ShapeDtypeStruct((m, d), out_dtype),
        compiler_params=_cparams(("parallel",)),
        name="rmsnorm",
    )(x, g.reshape(1, d))


def _stream_init_kernel(x_ref, xb_ref, ssq_ref):
    x = x_ref[...]
    xb_ref[...] = x.astype(BF16)
    ssq_ref[...] = jnp.broadcast_to(jnp.sum(x * x, axis=-1, keepdims=True), ssq_ref.shape)


def _stream_init(x, tm=256):
    m, d = x.shape
    return pl.pallas_call(
        _stream_init_kernel,
        grid=(m // tm,),
        in_specs=[pl.BlockSpec((tm, d), lambda i: (i, 0))],
        out_specs=[pl.BlockSpec((tm, d), lambda i: (i, 0)), pl.BlockSpec((tm, LANES), lambda i: (i, 0))],
        out_shape=[jax.ShapeDtypeStruct((m, d), BF16), jax.ShapeDtypeStruct((m, LANES), F32)],
        compiler_params=_cparams(("parallel",)),
        name="stream_init",
    )(x)


def _row_rscale(ssq_ref, d):
    return lax.rsqrt(ssq_ref[:, 0:1] * (1.0 / d) + EPS)


def _emit_stream(x_new, j_first, o_ref, xb_ref, ssq_ref):
    o_ref[...] = x_new
    xb_ref[...] = x_new.astype(BF16)
    part = jnp.broadcast_to(jnp.sum(x_new * x_new, axis=-1, keepdims=True), ssq_ref.shape)

    @pl.when(j_first)
    def _():
        ssq_ref[...] = part

    @pl.when(jnp.logical_not(j_first))
    def _():
        ssq_ref[...] += part


def _mm_normed_kernel(a_ref, ssq_ref, b_ref, o_ref, *, epilogue):
    acc = jnp.dot(a_ref[...], b_ref[...], preferred_element_type=F32) * _row_rscale(ssq_ref, a_ref.shape[1])
    if epilogue == "relu2":
        acc = jnp.square(jnp.maximum(acc, 0.0))
    o_ref[...] = acc.astype(o_ref.dtype)


def _matmul_normed(xb, ssq, b, layer, *, tm, tn, out_dtype, epilogue="none", name="matmul"):
    m, k = xb.shape
    n = b.shape[2]
    return pl.pallas_call(
        functools.partial(_mm_normed_kernel, epilogue=epilogue),
        grid=(m // tm, n // tn),
        in_specs=[pl.BlockSpec((tm, k), lambda i, j: (i, 0)), pl.BlockSpec((tm, LANES), lambda i, j: (i, 0)),
                  pl.BlockSpec((None, k, tn), lambda i, j: (layer, 0, j))],
        out_specs=pl.BlockSpec((tm, tn), lambda i, j: (i, j)),
        out_shape=jax.ShapeDtypeStruct((m, n), out_dtype),
        compiler_params=_cparams(("parallel", "parallel")),
        name=name,
    )(xb, ssq, b)


def _stream_out(m, n, tm, tn):
    tile = lambda *g: (g[0], g[1])
    rows = lambda *g: (g[0], 0)
    return ([pl.BlockSpec((tm, tn), tile), pl.BlockSpec((tm, tn), tile), pl.BlockSpec((tm, LANES), rows)],
            [jax.ShapeDtypeStruct((m, n), F32), jax.ShapeDtypeStruct((m, n), BF16),
             jax.ShapeDtypeStruct((m, LANES), F32)])


def _mm_res_stream_kernel(a_ref, b_ref, r_ref, o_ref, xb_ref, ssq_ref):
    x_new = r_ref[...] + jnp.dot(a_ref[...], b_ref[...], preferred_element_type=F32)
    _emit_stream(x_new, pl.program_id(1) == 0, o_ref, xb_ref, ssq_ref)


def _matmul_residual_stream(a, b, layer, residual, *, tm, tn, name="matmul_res"):
    m, k = a.shape
    n = b.shape[2]
    out_specs, out_shape = _stream_out(m, n, tm, tn)
    return pl.pallas_call(
        _mm_res_stream_kernel,
        grid=(m // tm, n // tn),
        in_specs=[pl.BlockSpec((tm, k), lambda i, j: (i, 0)),
                  pl.BlockSpec((None, k, tn), lambda i, j: (layer, 0, j)),
                  pl.BlockSpec((tm, tn), lambda i, j: (i, j))],
        out_specs=out_specs,
        out_shape=out_shape,
        compiler_params=_cparams(("parallel", "arbitrary")),
        name=name,
    )(a, b, residual)


def _mm_ktiled_res_kernel(a_ref, b_ref, r_ref, o_ref, xb_ref, ssq_ref, acc_ref):
    k = pl.program_id(2)

    @pl.when(k == 0)
    def _():
        acc_ref[...] = r_ref[...]

    acc_ref[...] += jnp.dot(a_ref[...], b_ref[...], preferred_element_type=F32)

    @pl.when(k == pl.num_programs(2) - 1)
    def _():
        _emit_stream(acc_ref[...], pl.program_id(1) == 0, o_ref, xb_ref, ssq_ref)


def _matmul_ktiled_residual(a, b, layer, residual, *, tm, tn, tk, name="matmul_k"):
    m, kk = a.shape
    n = b.shape[2]
    out_specs, out_shape = _stream_out(m, n, tm, tn)
    return pl.pallas_call(
        _mm_ktiled_res_kernel,
        grid=(m // tm, n // tn, kk // tk),
        in_specs=[pl.BlockSpec((tm, tk), lambda i, j, k: (i, k)),
                  pl.BlockSpec((None, tk, tn), lambda i, j, k: (layer, k, j)),
                  pl.BlockSpec((tm, tn), lambda i, j, k: (i, j))],
        out_specs=out_specs,
        out_shape=out_shape,
        scratch_shapes=[pltpu.VMEM((tm, tn), F32)],
        compiler_params=_cparams(("parallel", "arbitrary", "arbitrary")),
        name=name,
    )(a, b, residual)


A_AUG = A_DV + LANES


def _mlstm_kernel(qf_ref, kf_ref, vf_ref, gf_ref, qb_ref, kb_ref, vb_ref, gb_ref, bias_ref,
                  hf_ref, hb_ref, st_ref, m_ref):
    n = pl.program_id(1)

    @pl.when(n == 0)
    def _():
        st_ref[...] = jnp.zeros_like(st_ref)
        m_ref[...] = jnp.zeros_like(m_ref)

    lane = lax.broadcasted_iota(jnp.int32, (CHUNK, LANES), 1)
    ones_tile = jnp.where(lane == 0, 1.0, 0.0).astype(F32)
    kscale = A_DK ** -0.5

    probs = []
    for d, (q_ref, k_ref, v_ref, g_ref, h_ref) in enumerate(
            ((qf_ref, kf_ref, vf_ref, gf_ref, hf_ref), (qb_ref, kb_ref, vb_ref, gb_ref, hb_ref))):
        rev = d == 1
        incl, _ = _tri_masks(rev)
        cum = jnp.where(incl, 1.0, 0.0).astype(BF16)
        g = g_ref[...] + bias_ref[...]
        ls = _log_sigmoid(g)
        f_col = _mm_exact_lhs(cum, ls)
        f_row = f_col.T
        g_row = g.T
        for h in range(A_HEADS):
            ci = MLSTM_GATE_LANE + d * A_HEADS + h
            cf = MLSTM_GATE_LANE + 2 * A_HEADS + d * A_HEADS + h
            j = d * A_HEADS + h
            probs.append(dict(
                j=j, h=h, h_ref=h_ref, incl=incl, last=0 if rev else CHUNK - 1,
                q=q_ref[:, h * A_DK:(h + 1) * A_DK].astype(BF16),
                k=(k_ref[:, h * A_DK:(h + 1) * A_DK] * kscale).astype(BF16),
                v_aug=jnp.concatenate([v_ref[:, h * A_DV:(h + 1) * A_DV], ones_tile], axis=1),
                fc=f_col[:, cf:cf + 1], fr=f_row[cf:cf + 1, :], ir=g_row[ci:ci + 1, :], ic=g[:, ci:ci + 1],
                m_prev=m_ref[j:j + 1, 0:1], st=st_ref[j]))

    for grp in (probs,):
        for p in grp:
            p["qk"] = _mm_nt(p["q"], p["k"])
            p["q_st"] = _mm(p["q"], p["st"])
        for p in grp:
            d_log = jnp.where(p["incl"], p["fc"] - p["fr"] + p["ir"], -jnp.inf)
            inter = p["fc"] + p["m_prev"]
            p["m_t"] = jnp.maximum(inter, jnp.max(d_log, axis=-1, keepdims=True))
            p["s"] = p["qk"] * jnp.exp(d_log - p["m_t"])
            p["w_inter"] = jnp.exp(inter - p["m_t"])
        for p in grp:
            h = p["h"]
            num = _mm(p["s"], p["v_aug"]) + p["w_inter"] * p["q_st"]
            den = num[:, A_DV:A_DV + 1]
            p["h_ref"][:, h * A_DV:(h + 1) * A_DV] = (
                num[:, :A_DV] / jnp.maximum(jnp.abs(den), jnp.exp(-p["m_t"])))
        for p in grp:
            j, last = p["j"], p["last"]
            m_new = p["m_t"][last:last + 1, :]
            f_last = p["fc"][last:last + 1, :]
            w_k = jnp.exp(f_last - p["fc"] + p["ic"] - m_new)
            a = jnp.exp(f_last + p["m_prev"] - m_new)
            st_ref[j] = a * p["st"] + _mm_tn(p["k"], p["v_aug"] * w_k)
            m_ref[j:j + 1, :] = jnp.broadcast_to(m_new, (1, LANES))


def _chunk_rows(nchunk):
    return (lambda n: n), (lambda n: nchunk - 1 - n)


def _mlstm(proj, gate_bias):
    bsz, seq, _ = proj.shape
    fwd, bwd = _chunk_rows(seq // CHUNK)

    def specs(row):
        return [pl.BlockSpec((None, CHUNK, A_QK), lambda b, n: (b, row(n), OFF["aq"] // A_QK)),
                pl.BlockSpec((None, CHUNK, A_QK), lambda b, n: (b, row(n), OFF["ak"] // A_QK)),
                pl.BlockSpec((None, CHUNK, A_V), lambda b, n: (b, row(n), OFF["av"] // A_V)),
                pl.BlockSpec((None, CHUNK, LANES), lambda b, n: (b, row(n), OFF["gates"] // LANES))]

    nprob = 2 * A_HEADS
    return pl.pallas_call(
        _mlstm_kernel,
        grid=(bsz, seq // CHUNK),
        in_specs=specs(fwd) + specs(bwd) + [pl.BlockSpec((1, LANES), lambda b, n: (0, 0))],
        out_specs=[pl.BlockSpec((None, CHUNK, A_V), lambda b, n: (b, fwd(n), 0)),
                   pl.BlockSpec((None, CHUNK, A_V), lambda b, n: (b, bwd(n), 0))],
        out_shape=[jax.ShapeDtypeStruct((bsz, seq, A_V), F32)] * 2,
        scratch_shapes=[pltpu.VMEM((nprob, A_DK, A_AUG), F32), pltpu.VMEM((nprob, LANES), F32)],
        compiler_params=_cparams(("parallel", "arbitrary")),
        name="mlstm",
    )(proj, proj, proj, proj, proj, proj, proj, proj, gate_bias)


HALO = 8


def _conv_kernel(cur_ref, prev_ref, nxt_ref, w_ref, b_ref, o_ref, buf_ref):
    i = pl.program_id(1)
    tc = cur_ref.shape[0]
    first = i == 0
    last = i == pl.num_programs(1) - 1
    buf_ref[0:HALO, :] = jnp.where(first, 0.0, prev_ref[...])
    buf_ref[HALO:HALO + tc, :] = cur_ref[...]
    buf_ref[HALO + tc:HALO + tc + HALO, :] = jnp.where(last, 0.0, nxt_ref[...])
    pad = CONV_K // 2
    qscale = B_DK ** -0.5
    for c in range(B_QKV // LANES):
        cols = slice(c * LANES, (c + 1) * LANES)
        y = b_ref[:, cols]
        for j in range(CONV_K):
            y = y + w_ref[j:j + 1, cols] * buf_ref[HALO - pad + j:HALO - pad + j + tc, cols]
        y = y * _sigmoid(y)
        if c < 2 * B_HEADS:
            y = y * lax.rsqrt(jnp.sum(y * y, axis=-1, keepdims=True) + EPS)
            if c < B_HEADS:
                y = y * qscale
        o_ref[:, cols] = y


def _short_conv_qkv(proj, conv_w, conv_b, bsz, seq, tc=256):
    nt = seq // tc
    hb = tc // HALO
    nrow8 = bsz * seq // HALO
    cur = lambda b, i: (b * nt + i, 0)
    prev = lambda b, i: (jnp.maximum((b * nt + i) * hb - 1, 0), 0)
    nxt = lambda b, i: (jnp.minimum((b * nt + i + 1) * hb, nrow8 - 1), 0)
    w8 = jnp.zeros((8, B_QKV), F32).at[:CONV_K].set(conv_w)
    return pl.pallas_call(
        _conv_kernel,
        grid=(bsz, nt),
        in_specs=[pl.BlockSpec((tc, B_QKV), cur), pl.BlockSpec((HALO, B_QKV), prev),
                  pl.BlockSpec((HALO, B_QKV), nxt),
                  pl.BlockSpec((8, B_QKV), lambda b, i: (0, 0)), pl.BlockSpec((1, B_QKV), lambda b, i: (0, 0))],
        out_specs=pl.BlockSpec((tc, B_QKV), cur),
        out_shape=jax.ShapeDtypeStruct((bsz * seq, B_QKV), F32),
        scratch_shapes=[pltpu.VMEM((tc + 2 * HALO, B_QKV), F32)],
        compiler_params=_cparams(("parallel", "arbitrary")),
        name="short_conv",
    )(proj, proj, proj, w8, conv_b.reshape(1, B_QKV))


def _gdn_kernel(qkvf_ref, gf_ref, qkvb_ref, gb_ref, alog_ref, dtb_ref, of_ref, ob_ref, s_ref):
    n = pl.program_id(0)
    bsz = qkvf_ref.shape[0]

    @pl.when(n == 0)
    def _():
        s_ref[...] = jnp.zeros_like(s_ref)

    r_i = lax.broadcasted_iota(jnp.int32, (CHUNK, CHUNK), 0)
    c_i = lax.broadcasted_iota(jnp.int32, (CHUNK, CHUNK), 1)
    eye = jnp.where(r_i == c_i, 1.0, 0.0).astype(F32)

    probs = []
    for b, (d, (x_ref, g_ref, o_ref)) in itertools.product(
            range(bsz), enumerate(((qkvf_ref, gf_ref, of_ref), (qkvb_ref, gb_ref, ob_ref)))):
        rev = d == 1
        incl, strict = _tri_masks(rev)
        cum = jnp.where(incl, 1.0, 0.0).astype(BF16)
        gt = g_ref[b]
        dec = -jnp.exp(alog_ref[...]) * _softplus(gt + dtb_ref[...])
        beta = _sigmoid(gt)
        g_col = _mm_exact_lhs(cum, dec)
        g_row = g_col.T
        e_col = jnp.exp(g_col)
        for h in range(B_HEADS):
            cd = GDN_GATE_LANE + d * B_HEADS + h
            cb = GDN_GATE_LANE + 2 * B_HEADS + d * B_HEADS + h
            probs.append(dict(
                j=(b * 2 + d) * B_HEADS + h, b=b, h=h, o_ref=o_ref, incl=incl, strict=strict,
                last=0 if rev else CHUNK - 1,
                q=x_ref[b, :, h * B_DK:(h + 1) * B_DK].astype(BF16),
                k=x_ref[b, :, B_QK + h * B_DK:B_QK + (h + 1) * B_DK],
                v=x_ref[b, :, 2 * B_QK + h * B_DV:2 * B_QK + (h + 1) * B_DV],
                gc=g_col[:, cd:cd + 1], gr=g_row[cd:cd + 1, :], ec=e_col[:, cd:cd + 1], bc=beta[:, cb:cb + 1]))

    for p in probs:
        kb = p["k"].astype(BF16)
        p["kq"] = _mm_nt(jnp.concatenate([kb, p["q"]], axis=0), kb)
    for p in probs:
        e = jnp.exp(jnp.where(p["incl"], p["gc"] - p["gr"], -jnp.inf))
        p["a_qk"] = p["kq"][CHUNK:] * e
        p["lmat"] = (p["bc"] * p["kq"][:CHUNK]) * jnp.where(p["strict"], e, 0.0)
    same_block = lambda s: (r_i // s) == (c_i // s)
    for p in probs:
        p["t"] = eye - jnp.where(same_block(2), p["lmat"], 0.0)
    s = 2
    while s < CHUNK:
        couple = same_block(2 * s) & jnp.logical_not(same_block(s))
        for p in probs:
            p["cd"] = _mm(jnp.where(couple, p["lmat"], 0.0), p["t"])
        for p in probs:
            p["t"] = p["t"] - _mm(p["t"], p["cd"])
        s *= 2
    for p in probs:
        rhs = jnp.concatenate([(p["bc"] * p["ec"]) * p["k"], p["bc"] * p["v"]], axis=1)
        p["wu"] = _mm(p["t"], rhs)
    for p in probs:
        p["s"] = s_ref[p["j"]]
        p["wq_s"] = _mm(jnp.concatenate([p["wu"][:, :B_DK].astype(BF16), p["q"]], axis=0), p["s"])
    for p in probs:
        p["uu"] = p["wu"][:, B_DK:] - p["wq_s"][:CHUNK]
    for p in probs:
        h = p["h"]
        p["o_ref"][p["b"], :, h * B_DV:(h + 1) * B_DV] = p["ec"] * p["wq_s"][CHUNK:] + _mm(p["a_qk"], p["uu"])
    for p in probs:
        g_last = p["gc"][p["last"]:p["last"] + 1, :]
        k_dec = p["k"] * jnp.exp(g_last - p["gc"])
        s_ref[p["j"]] = jnp.exp(g_last) * p["s"] + _mm_tn(k_dec, p["uu"])


def _gdn(qkv, proj, a_log, dt_bias):
    bsz, seq, _ = proj.shape
    fwd, bwd = _chunk_rows(seq // CHUNK)
    vec = lambda: pl.BlockSpec((1, LANES), lambda n: (0, 0))
    return pl.pallas_call(
        _gdn_kernel,
        grid=(seq // CHUNK,),
        in_specs=[pl.BlockSpec((bsz, CHUNK, B_QKV), lambda n: (0, fwd(n), 0)),
                  pl.BlockSpec((bsz, CHUNK, LANES), lambda n: (0, fwd(n), OFF["gates"] // LANES)),
                  pl.BlockSpec((bsz, CHUNK, B_QKV), lambda n: (0, bwd(n), 0)),
                  pl.BlockSpec((bsz, CHUNK, LANES), lambda n: (0, bwd(n), OFF["gates"] // LANES)),
                  vec(), vec()],
        out_specs=[pl.BlockSpec((bsz, CHUNK, B_V), lambda n: (0, fwd(n), 0)),
                   pl.BlockSpec((bsz, CHUNK, B_V), lambda n: (0, bwd(n), 0))],
        out_shape=[jax.ShapeDtypeStruct((bsz, seq, B_V), F32)] * 2,
        scratch_shapes=[pltpu.VMEM((bsz * 2 * B_HEADS, B_DK, B_DV), F32)],
        compiler_params=_cparams(("arbitrary",)),
        name="gdn",
    )(qkv, proj, qkv, proj, a_log, dt_bias)


SUB = 16


def _gla_kernel(qf_ref, kf_ref, vf_ref, rf_ref, qb_ref, kb_ref, vb_ref, rb_ref, wg_ref, bg_ref,
                of_ref, ob_ref, s_ref):
    n = pl.program_id(0)
    bsz = qf_ref.shape[0]

    @pl.when(n == 0)
    def _():
        s_ref[...] = jnp.zeros_like(s_ref)

    qscale = C_DK ** -0.5
    row = lax.broadcasted_iota(jnp.int32, (CHUNK, C_DK), 0)
    nsub = CHUNK // SUB
    probs = []
    for b, (d, (q_ref, k_ref, v_ref, r_ref, o_ref)) in itertools.product(range(bsz), enumerate(
            ((qf_ref, kf_ref, vf_ref, rf_ref, of_ref), (qb_ref, kb_ref, vb_ref, rb_ref, ob_ref)))):
        rev = d == 1
        incl, _ = _tri_masks(rev)
        cum = jnp.where(incl, 1.0, 0.0).astype(BF16)
        z = _mm_x3(r_ref[b], wg_ref[:, d * C_QK:(d + 1) * C_QK]) + bg_ref[:, d * C_QK:(d + 1) * C_QK]
        lg = _log_sigmoid(z) / GLA_TAU
        b_all = _mm_exact_lhs(cum, lg)
        for h in range(C_HEADS):
            j = (b * 2 + d) * C_HEADS + h
            probs.append(dict(
                j=j, bi=b, h=h, o_ref=o_ref, rev=rev, incl=incl, last=0 if rev else CHUNK - 1,
                q=q_ref[b, :, h * C_DK:(h + 1) * C_DK] * qscale, k=k_ref[b, :, h * C_DK:(h + 1) * C_DK],
                v=v_ref[b, :, h * C_DV:(h + 1) * C_DV].astype(BF16), b=b_all[:, h * C_DK:(h + 1) * C_DK],
                s_t=s_ref[j]))

    for p in probs:
        q, k, b, rev = p["q"], p["k"], p["b"], p["rev"]
        k_slots, in_blocks = [], []
        b_ref_rows = b
        for i in range(nsub):
            ref_row = (i + 1) * SUB - 1 if rev else i * SUB
            b_ref_row = b[ref_row:ref_row + 1, :]
            in_blocks.append((row >= i * SUB) & (row < (i + 1) * SUB))
            b_ref_rows = jnp.where(in_blocks[i], b_ref_row, b_ref_rows)
            visible = (row >= i * SUB) if rev else (row < (i + 1) * SUB)
            k_slots.append((k * jnp.exp(jnp.where(visible, b_ref_row - b, -jnp.inf))).astype(BF16))
        q_dec = q * jnp.exp(b - b_ref_rows)
        q_slots = [jnp.where(m, q_dec, 0.0).astype(BF16) for m in in_blocks]
        p["att"] = _mm_nt(jnp.concatenate(q_slots, axis=1), jnp.concatenate(k_slots, axis=1))
        p["q_s"] = _mm_nt(q * jnp.exp(b), p["s_t"])
    for p in probs:
        h = p["h"]
        att = jnp.where(p["incl"], p["att"], 0.0)
        p["o_ref"][p["bi"], :, h * C_DV:(h + 1) * C_DV] = _mm(att, p["v"]) + p["q_s"]
    for p in probs:
        b, last = p["b"], p["last"]
        b_last = b[last:last + 1, :]
        s_ref[p["j"]] = jnp.exp(b_last) * p["s_t"] + _mm_tn(p["v"], p["k"] * jnp.exp(b_last - b))


def _gla(proj, w_gate, b_gate):
    bsz, seq, _ = proj.shape
    fwd, bwd = _chunk_rows(seq // CHUNK)

    def specs(row):
        return [pl.BlockSpec((bsz, CHUNK, C_QK), lambda n: (0, row(n), OFF["cq"] // C_QK)),
                pl.BlockSpec((bsz, CHUNK, C_QK), lambda n: (0, row(n), OFF["ck"] // C_QK)),
                pl.BlockSpec((bsz, CHUNK, C_V), lambda n: (0, row(n), OFF["cv"] // C_V)),
                pl.BlockSpec((bsz, CHUNK, LANES), lambda n: (0, row(n), OFF["gates"] // LANES))]

    return pl.pallas_call(
        _gla_kernel,
        grid=(seq // CHUNK,),
        in_specs=specs(fwd) + specs(bwd) + [pl.BlockSpec((LANES, 2 * C_QK), lambda n: (0, 0)),
                                            pl.BlockSpec((1, 2 * C_QK), lambda n: (0, 0))],
        out_specs=[pl.BlockSpec((bsz, CHUNK, C_V), lambda n: (0, fwd(n), 0)),
                   pl.BlockSpec((bsz, CHUNK, C_V), lambda n: (0, bwd(n), 0))],
        out_shape=[jax.ShapeDtypeStruct((bsz, seq, C_V), F32)] * 2,
        scratch_shapes=[pltpu.VMEM((bsz * 2 * C_HEADS, C_DV, C_DK), F32)],
        compiler_params=_cparams(("arbitrary",)),
        name="gla",
    )(proj, proj, proj, proj, proj, proj, proj, proj, w_gate, b_gate)


def _post_kernel(hf_ref, hb_ref, z_ref, g_ref, o_ref, *, head_dim, gate):
    nheads = hf_ref.shape[1] // head_dim
    for h in range(nheads):
        cols = slice(h * head_dim, (h + 1) * head_dim)
        x = hf_ref[:, cols] + hb_ref[:, cols]
        y = x * lax.rsqrt(jnp.mean(x * x, axis=-1, keepdims=True) + EPS) * g_ref[:, cols]
        z = z_ref[:, cols]
        sg = _sigmoid(z)
        o_ref[:, cols] = (y * (sg if gate == "sigmoid" else z * sg)).astype(o_ref.dtype)


def _post(hf, hb, proj, z_off, gain, head_dim, gate, tm=256):
    m, w = hf.shape
    return pl.pallas_call(
        functools.partial(_post_kernel, head_dim=head_dim, gate=gate),
        grid=(m // tm,),
        in_specs=[pl.BlockSpec((tm, w), lambda i: (i, 0)), pl.BlockSpec((tm, w), lambda i: (i, 0)),
                  pl.BlockSpec((tm, w), lambda i: (i, z_off // w)), pl.BlockSpec((1, w), lambda i: (0, 0))],
        out_specs=pl.BlockSpec((tm, w), lambda i: (i, 0)),
        out_shape=jax.ShapeDtypeStruct((m, w), BF16),
        compiler_params=_cparams(("parallel",)),
        name="head_norm_gate",
    )(hf, hb, proj, gain.reshape(1, w))


def _merge_kernel(ya_ref, yb_ref, yc_ref, gh_ref, wa_ref, wb_ref, wc_ref, wg_ref, bg_ref, o_ref):
    gh = gh_ref[...].astype(BF16)
    acc = None
    for j, (y_ref, w_ref) in enumerate(((ya_ref, wa_ref), (yb_ref, wb_ref), (yc_ref, wc_ref))):
        gate = _sigmoid(jnp.dot(gh, wg_ref[j], preferred_element_type=F32) + bg_ref[j])
        term = gate * jnp.dot(y_ref[...], w_ref[...], preferred_element_type=F32)
        acc = term if acc is None else acc + term
    o_ref[...] = acc.astype(o_ref.dtype)


def _merge(ya, yb, yc, proj, w_branch, w_gate, b_gate, layer, tm=512, tn=512):
    m, kb = ya.shape
    n = w_gate.shape[-1]
    ysp = lambda: pl.BlockSpec((tm, kb), lambda i, j: (i, 0))
    wsp = lambda: pl.BlockSpec((None, kb, tn), lambda i, j: (layer, 0, j))
    return pl.pallas_call(
        _merge_kernel,
        grid=(m // tm, n // tn),
        in_specs=[ysp(), ysp(), ysp(),
                  pl.BlockSpec((tm, GATE_RANK), lambda i, j: (i, OFF["gh"] // GATE_RANK)),
                  wsp(), wsp(), wsp(),
                  pl.BlockSpec((None, 3, GATE_RANK, tn), lambda i, j: (layer, 0, 0, j)),
                  pl.BlockSpec((None, 3, 1, tn), lambda i, j: (layer, 0, 0, j))],
        out_specs=pl.BlockSpec((tm, tn), lambda i, j: (i, j)),
        out_shape=jax.ShapeDtypeStruct((m, n), BF16),
        compiler_params=_cparams(("parallel", "parallel")),
        name="merge",
    )(ya, yb, yc, proj, *w_branch, w_gate, b_gate)


def _cast_kernel(x_ref, o_ref):
    o_ref[...] = x_ref[...].astype(o_ref.dtype)


def _scale_cast_kernel(x_ref, g_ref, o_ref):
    o_ref[...] = (x_ref[...] * g_ref[...]).astype(o_ref.dtype)


def _cast_bf16(w, row_gain=None):
    shape = w.shape
    lead, (rows, cols) = shape[:-2], shape[-2:]
    tr, tc = _pick_tile(rows, (1024, 512, 256, 128, 64, 32, 16)), _pick_tile(cols, (2048, 1024, 512, 256, 128))
    nl = len(lead)
    spec = pl.BlockSpec((None,) * nl + (tr, tc), lambda *g: g)
    in_specs, args, body = [spec], [w], _cast_kernel
    if row_gain is not None:
        in_specs.append(pl.BlockSpec((None,) * nl + (tr, 1), lambda *g: g[:-1] + (0,)))
        args.append(row_gain.astype(F32).reshape(lead + (rows, 1)))
        body = _scale_cast_kernel
    return pl.pallas_call(
        body,
        grid=lead + (rows // tr, cols // tc),
        in_specs=in_specs,
        out_specs=spec,
        out_shape=jax.ShapeDtypeStruct(shape, BF16),
        compiler_params=_cparams(("parallel",) * (nl + 2)),
        name="cast_bf16",
    )(*args)


_GATE_SEGS = ("agt", "bgt", "clr")


def _permute_w_in_kernel(x_ref, g_ref, o_ref):
    gain = g_ref[...]
    for (start, w), name in zip(_PIECES, _NEW_ORDER):
        if name not in _GATE_SEGS:
            o_ref[:, OFF[name]:OFF[name] + w] = (x_ref[:, start:start + w] * gain).astype(BF16)
    lane = lax.broadcasted_iota(jnp.int32, (x_ref.shape[0], LANES), 1)
    tile = jnp.zeros((x_ref.shape[0], LANES), F32)
    for (start, w), name in zip(_PIECES, _NEW_ORDER):
        if name in _GATE_SEGS:
            lane0 = OFF[name] - OFF["gates"]
            src0 = start - lane0
            assert src0 % LANES == 0
            tile = jnp.where((lane >= lane0) & (lane < lane0 + w), x_ref[:, src0:src0 + LANES], tile)
    o_ref[:, OFF["gates"]:OFF["gates"] + LANES] = (tile * gain).astype(BF16)
    if PROJ_W > OFF["gates"] + LANES:
        o_ref[:, OFF["gates"] + LANES:] = jnp.zeros((x_ref.shape[0], PROJ_W - OFF["gates"] - LANES), BF16)


def _permute_w_in(w_in, row_gain):
    depth, d, width = w_in.shape
    tr = _pick_tile(d, (256, 128, 64, 32, 16))
    return pl.pallas_call(
        _permute_w_in_kernel,
        grid=(depth, d // tr),
        in_specs=[pl.BlockSpec((None, tr, width), lambda l, i: (l, i, 0)),
                  pl.BlockSpec((None, tr, 1), lambda l, i: (l, i, 0))],
        out_specs=pl.BlockSpec((None, tr, PROJ_W), lambda l, i: (l, i, 0)),
        out_shape=jax.ShapeDtypeStruct((depth, d, PROJ_W), BF16),
        compiler_params=_cparams(("parallel", "parallel")),
        name="permute_w_in",
    )(w_in, row_gain.astype(F32).reshape(depth, d, 1))


def _pad_lanes(v, lane0):
    return jnp.zeros(v.shape[:-1] + (LANES,), F32).at[..., lane0:lane0 + v.shape[-1]].set(v.astype(F32))


def _gla_gate_weights(w, b):
    depth = w.shape[0]
    wg = jnp.zeros((depth, LANES, 2 * C_QK), F32)
    for n in range(2):
        r0 = GLA_GATE_LANE + n * GLA_RANK
        wg = wg.at[:, r0:r0 + GLA_RANK, n * C_QK:(n + 1) * C_QK].set(w[:, n].astype(F32))
    return wg, b.astype(F32).reshape(depth, 1, 2 * C_QK)


def _pick_tile(n, prefs):
    for t in prefs:
        if n % t == 0:
            return t
    return n


def kernel(x, norm1_g, w_in, conv_w, conv_b, mlstm_gate_b, mlstm_norm_g, gdn_a_log, gdn_dt_bias, gdn_norm_g, gla_w_gate, gla_b_gate, gla_norm_g, w_branch_a, w_branch_b, w_branch_c, w_merge_gate, b_merge_gate, w_out, norm2_g, w_ff1, w_ff2, final_g):
    bsz, seq, d_model = x.shape
    depth = w_in.shape[0]
    m = bsz * seq
    assert seq % CHUNK == 0 and m % 256 == 0

    w_in_p = _permute_w_in(w_in, norm1_g)
    w_br = [_cast_bf16(w) for w in (w_branch_a, w_branch_b, w_branch_c)]
    w_mg = _cast_bf16(w_merge_gate)
    b_mg = b_merge_gate.astype(F32).reshape(depth, 3, 1, d_model)
    w_out_b = _cast_bf16(w_out)
    w_ff1_b = _cast_bf16(w_ff1, row_gain=norm2_g)
    w_ff2_b = _cast_bf16(w_ff2)
    gate_bias_a = _pad_lanes(mlstm_gate_b.reshape(depth, 1, 4 * A_HEADS), MLSTM_GATE_LANE)
    a_log = _pad_lanes(gdn_a_log.reshape(depth, 1, 2 * B_HEADS), GDN_GATE_LANE)
    dt_bias = _pad_lanes(gdn_dt_bias.reshape(depth, 1, 2 * B_HEADS), GDN_GATE_LANE)
    gla_wg, gla_bg = _gla_gate_weights(gla_w_gate, gla_b_gate)

    tm = _pick_tile(m, (1024, 512, 256))
    d_ff = w_ff1.shape[-1]
    xs = x.reshape(m, d_model).astype(F32)
    xb, ssq = _stream_init(xs)
    for l in range(depth):
        proj = _matmul_normed(xb, ssq, w_in_p, l, tm=tm, tn=PROJ_TILE, out_dtype=F32, name="in_proj")

        proj3 = proj.reshape(bsz, seq, PROJ_W)
        flat = lambda hs: [h.reshape(m, h.shape[-1]) for h in hs]
        ha_f, ha_b = flat(_mlstm(proj3, gate_bias_a[l]))
        qkv = _short_conv_qkv(proj, conv_w[l], conv_b[l], bsz, seq)
        hb_f, hb_b = flat(_gdn(qkv.reshape(bsz, seq, B_QKV), proj3, a_log[l], dt_bias[l]))
        hc_f, hc_b = flat(_gla(proj3, gla_wg[l], gla_bg[l]))

        ya = _post(ha_f, ha_b, proj, OFF["ao"], mlstm_norm_g[l], A_DV, "sigmoid")
        yb = _post(hb_f, hb_b, proj, OFF["bz"], gdn_norm_g[l], B_DV, "silu")
        yc = _post(hc_f, hc_b, proj, OFF["cr"], gla_norm_g[l], C_DV, "silu")
        mix = _merge(ya, yb, yc, proj, w_br, w_mg, b_mg, l, tm=tm, tn=_pick_tile(d_model, (512, 256, 128)))
        xs, xb, ssq = _matmul_residual_stream(mix, w_out_b, l, xs, tm=tm, tn=_pick_tile(d_model, (512, 256, 128)),
                                              name="out_proj")

        hid = _matmul_normed(xb, ssq, w_ff1_b, l, tm=tm, tn=_pick_tile(d_ff, (1024, 512, 256, 128)),
                             out_dtype=BF16, epilogue="relu2", name="ff1")
        xs, xb, ssq = _matmul_ktiled_residual(hid, w_ff2_b, l, xs, tm=tm,
                                              tn=_pick_tile(d_model, (1024, 512, 256, 128)),
                                              tk=_pick_tile(d_ff, (2048, 1024, 512, 256, 128)), name="ff2")
    out = _rmsnorm(xs, final_g, x.dtype)
    return out.reshape(bsz, seq, d_model)
```

```python
import functools
import itertools

import numpy as np
import jax
import jax.numpy as jnp
from jax import lax
from jax.experimental import pallas as pl
from jax.experimental.pallas import tpu as pltpu

F32 = jnp.float32
BF16 = jnp.bfloat16

A_HEADS, A_DK, A_DV = 4, 128, 256
B_HEADS, B_DK, B_DV = 8, 128, 128
C_HEADS, C_DK, C_DV = 4, 128, 256
GLA_RANK = 16
GLA_TAU = 16.0
GATE_RANK = 256
CONV_K = 5
CHUNK = 64
EPS = 1e-6

A_QK, A_V = A_HEADS * A_DK, A_HEADS * A_DV
B_QK, B_V = B_HEADS * B_DK, B_HEADS * B_DV
B_QKV = 2 * B_QK + B_V
C_QK, C_V = C_HEADS * C_DK, C_HEADS * C_DV
PROJ_SIZES = (A_QK, A_QK, A_V, A_V, 4 * A_HEADS,
              B_QKV, B_V, 4 * B_HEADS,
              C_QK, C_QK, C_V, C_V, 2 * GLA_RANK,
              GATE_RANK)
_SEG_NAMES = ("aq", "ak", "av", "ao", "agt", "bqkv", "bz", "bgt", "cq", "ck", "cv", "cr", "clr", "gh")

LANES = 128
VMEM_LIMIT_BYTES = 56 * 1024 * 1024

_NEW_ORDER = ("bqkv", "bz", "aq", "ak", "av", "ao", "cq", "ck", "cv", "cr", "gh", "agt", "bgt", "clr")
PROJ_TILE = 512


def _layout():
    sizes = dict(zip(_SEG_NAMES, PROJ_SIZES))
    starts = dict(zip(_SEG_NAMES, np.cumsum((0,) + PROJ_SIZES)[:-1].tolist()))
    off, new_off, pieces = 0, {}, []
    for name in _NEW_ORDER:
        new_off[name] = off
        pieces.append((starts[name], sizes[name]))
        off += sizes[name]
    return new_off, pieces, off, off + (-off) % PROJ_TILE


OFF, _PIECES, _PROJ_USED, PROJ_W = _layout()
OFF["gates"] = OFF["agt"]
assert OFF["gates"] % LANES == 0 and _PROJ_USED - OFF["gates"] <= LANES
MLSTM_GATE_LANE = OFF["agt"] - OFF["gates"]
GDN_GATE_LANE = OFF["bgt"] - OFF["gates"]
GLA_GATE_LANE = OFF["clr"] - OFF["gates"]


def _cparams(sem):
    return pltpu.CompilerParams(dimension_semantics=sem, vmem_limit_bytes=VMEM_LIMIT_BYTES)


def _mm(a, b):
    return jnp.dot(a.astype(BF16), b.astype(BF16), preferred_element_type=F32)


def _mm_nt(a, b):
    return lax.dot_general(a.astype(BF16), b.astype(BF16), (((1,), (1,)), ((), ())),
                           preferred_element_type=F32)


def _mm_tn(a, b):
    return lax.dot_general(a.astype(BF16), b.astype(BF16), (((0,), (0,)), ((), ())),
                           preferred_element_type=F32)


def _split3(x):
    hi = x.astype(BF16)
    r = x - hi.astype(F32)
    mid = r.astype(BF16)
    lo = (r - mid.astype(F32)).astype(BF16)
    return hi, mid, lo


def _mm_exact_lhs(t_bf16, x):
    hi, mid, lo = _split3(x)
    d = functools.partial(jnp.dot, preferred_element_type=F32)
    return d(t_bf16, hi) + d(t_bf16, mid) + d(t_bf16, lo)


def _mm_x3(a, b):
    ah = a.astype(BF16)
    al = (a - ah.astype(F32)).astype(BF16)
    bh = b.astype(BF16)
    bl = (b - bh.astype(F32)).astype(BF16)
    d = functools.partial(jnp.dot, preferred_element_type=F32)
    return d(ah, bh) + d(ah, bl) + d(al, bh)


def _softplus(x):
    return jnp.maximum(x, 0.0) + jnp.log1p(jnp.exp(-jnp.abs(x)))


def _log_sigmoid(x):
    return -_softplus(-x)


def _sigmoid(x):
    return 1.0 / (1.0 + jnp.exp(-x))


def _tri_masks(rev):
    r = lax.broadcasted_iota(jnp.int32, (CHUNK, CHUNK), 0)
    c = lax.broadcasted_iota(jnp.int32, (CHUNK, CHUNK), 1)
    incl = (c >= r) if rev else (c <= r)
    strict = (c > r) if rev else (c < r)
    return incl, strict


def _rmsnorm_kernel(x_ref, g_ref, o_ref):
    x = x_ref[...]
    y = x * lax.rsqrt(jnp.mean(x * x, axis=-1, keepdims=True) + EPS)
    o_ref[...] = (y * g_ref[...]).astype(o_ref.dtype)


def _rmsnorm(x, g, out_dtype, tm=256):
    m, d = x.shape
    return pl.pallas_call(
        _rmsnorm_kernel,
        grid=(m // tm,),
        in_specs=[pl.BlockSpec((tm, d), lambda i: (i, 0)), pl.BlockSpec((1, d), lambda i: (0, 0))],
        out_specs=pl.BlockSpec((tm, d), lambda i: (i, 0)),
        out_shape=jax.ShapeDtypeStruct((m, d), out_dtype),
        compiler_params=_cparams(("parallel",)),
        name="rmsnorm",
    )(x, g.reshape(1, d))


def _stream_init_kernel(x_ref, xb_ref, ssq_ref):
    x = x_ref[...]
    xb_ref[...] = x.astype(BF16)
    ssq_ref[...] = jnp.broadcast_to(jnp.sum(x * x, axis=-1, keepdims=True), ssq_ref.shape)


def _stream_init(x, tm=256):
    m, d = x.shape
    return pl.pallas_call(
        _stream_init_kernel,
        grid=(m // tm,),
        in_specs=[pl.BlockSpec((tm, d), lambda i: (i, 0))],
        out_specs=[pl.BlockSpec((tm, d), lambda i: (i, 0)), pl.BlockSpec((tm, LANES), lambda i: (i, 0))],
        out_shape=[jax.ShapeDtypeStruct((m, d), BF16), jax.ShapeDtypeStruct((m, LANES), F32)],
        compiler_params=_cparams(("parallel",)),
        name="stream_init",
    )(x)


def _row_rscale(ssq_ref, d):
    return lax.rsqrt(ssq_ref[:, 0:1] * (1.0 / d) + EPS)


def _emit_stream(x_new, j_first, o_ref, xb_ref, ssq_ref):
    o_ref[...] = x_new
    xb_ref[...] = x_new.astype(BF16)
    part = jnp.broadcast_to(jnp.sum(x_new * x_new, axis=-1, keepdims=True), ssq_ref.shape)

    @pl.when(j_first)
    def _():
        ssq_ref[...] = part

    @pl.when(jnp.logical_not(j_first))
    def _():
        ssq_ref[...] += part


def _mm_normed_kernel(a_ref, ssq_ref, b_ref, o_ref, *, epilogue, b_transposed):
    dims = (((1,), (1 if b_transposed else 0,)), ((), ()))
    acc = lax.dot_general(a_ref[...], b_ref[...], dims, preferred_element_type=F32)
    acc = acc * _row_rscale(ssq_ref, a_ref.shape[1])
    if epilogue == "relu2":
        acc = jnp.square(jnp.maximum(acc, 0.0))
    o_ref[...] = acc.astype(o_ref.dtype)


def _matmul_normed(xb, ssq, b, layer, *, tm, tn, out_dtype, epilogue="none", b_transposed=False, name="matmul"):
    m, k = xb.shape
    n = b.shape[1] if b_transposed else b.shape[2]
    b_spec = (pl.BlockSpec((None, tn, k), lambda i, j: (layer, j, 0)) if b_transposed
              else pl.BlockSpec((None, k, tn), lambda i, j: (layer, 0, j)))
    return pl.pallas_call(
        functools.partial(_mm_normed_kernel, epilogue=epilogue, b_transposed=b_transposed),
        grid=(m // tm, n // tn),
        in_specs=[pl.BlockSpec((tm, k), lambda i, j: (i, 0)), pl.BlockSpec((tm, LANES), lambda i, j: (i, 0)),
                  b_spec],
        out_specs=pl.BlockSpec((tm, tn), lambda i, j: (i, j)),
        out_shape=jax.ShapeDtypeStruct((m, n), out_dtype),
        compiler_params=_cparams(("parallel", "parallel")),
        name=name,
    )(xb, ssq, b)


def _stream_out(m, n, tm, tn):
    tile = lambda *g: (g[0], g[1])
    rows = lambda *g: (g[0], 0)
    return ([pl.BlockSpec((tm, tn), tile), pl.BlockSpec((tm, tn), tile), pl.BlockSpec((tm, LANES), rows)],
            [jax.ShapeDtypeStruct((m, n), F32), jax.ShapeDtypeStruct((m, n), BF16),
             jax.ShapeDtypeStruct((m, LANES), F32)])


def _mm_res_stream_kernel(a_ref, b_ref, r_ref, o_ref, xb_ref, ssq_ref):
    x_new = r_ref[...] + jnp.dot(a_ref[...], b_ref[...], preferred_element_type=F32)
    _emit_stream(x_new, pl.program_id(1) == 0, o_ref, xb_ref, ssq_ref)


def _matmul_residual_stream(a, b, layer, residual, *, tm, tn, name="matmul_res"):
    m, k = a.shape
    n = b.shape[2]
    out_specs, out_shape = _stream_out(m, n, tm, tn)
    return pl.pallas_call(
        _mm_res_stream_kernel,
        grid=(m // tm, n // tn),
        in_specs=[pl.BlockSpec((tm, k), lambda i, j: (i, 0)),
                  pl.BlockSpec((None, k, tn), lambda i, j: (layer, 0, j)),
                  pl.BlockSpec((tm, tn), lambda i, j: (i, j))],
        out_specs=out_specs,
        out_shape=out_shape,
        compiler_params=_cparams(("parallel", "arbitrary")),
        name=name,
    )(a, b, residual)


def _mm_ktiled_res_kernel(a_ref, b_ref, r_ref, o_ref, xb_ref, ssq_ref, acc_ref):
    k = pl.program_id(2)

    @pl.when(k == 0)
    def _():
        acc_ref[...] = r_ref[...]

    acc_ref[...] += jnp.dot(a_ref[...], b_ref[...], preferred_element_type=F32)

    @pl.when(k == pl.num_programs(2) - 1)
    def _():
        _emit_stream(acc_ref[...], pl.program_id(1) == 0, o_ref, xb_ref, ssq_ref)


def _matmul_ktiled_residual(a, b, layer, residual, *, tm, tn, tk, name="matmul_k"):
    m, kk = a.shape
    n = b.shape[2]
    out_specs, out_shape = _stream_out(m, n, tm, tn)
    return pl.pallas_call(
        _mm_ktiled_res_kernel,
        grid=(m // tm, n // tn, kk // tk),
        in_specs=[pl.BlockSpec((tm, tk), lambda i, j, k: (i, k)),
                  pl.BlockSpec((None, tk, tn), lambda i, j, k: (layer, k, j)),
                  pl.BlockSpec((tm, tn), lambda i, j, k: (i, j))],
        out_specs=out_specs,
        out_shape=out_shape,
        scratch_shapes=[pltpu.VMEM((tm, tn), F32)],
        compiler_params=_cparams(("parallel", "arbitrary", "arbitrary")),
        name=name,
    )(a, b, residual)


A_AUG = A_DV + LANES


def _mlstm_kernel(qf_ref, kf_ref, vf_ref, gf_ref, qb_ref, kb_ref, vb_ref, gb_ref, bias_ref,
                  hf_ref, hb_ref, st_ref, m_ref):
    n = pl.program_id(1)

    @pl.when(n == 0)
    def _():
        st_ref[...] = jnp.zeros_like(st_ref)
        m_ref[...] = jnp.zeros_like(m_ref)

    lane = lax.broadcasted_iota(jnp.int32, (CHUNK, LANES), 1)
    ones_tile = jnp.where(lane == 0, 1.0, 0.0).astype(F32)
    kscale = A_DK ** -0.5

    probs = []
    for d, (q_ref, k_ref, v_ref, g_ref, h_ref) in enumerate(
            ((qf_ref, kf_ref, vf_ref, gf_ref, hf_ref), (qb_ref, kb_ref, vb_ref, gb_ref, hb_ref))):
        rev = d == 1
        incl, _ = _tri_masks(rev)
        cum = jnp.where(incl, 1.0, 0.0).astype(BF16)
        g = g_ref[...] + bias_ref[...]
        ls = _log_sigmoid(g)
        f_col = _mm_exact_lhs(cum, ls)
        f_row = f_col.T
        g_row = g.T
        for h in range(A_HEADS):
            ci = MLSTM_GATE_LANE + d * A_HEADS + h
            cf = MLSTM_GATE_LANE + 2 * A_HEADS + d * A_HEADS + h
            j = d * A_HEADS + h
            probs.append(dict(
                j=j, h=h, h_ref=h_ref, incl=incl, last=0 if rev else CHUNK - 1,
                q=q_ref[:, h * A_DK:(h + 1) * A_DK].astype(BF16),
                k=(k_ref[:, h * A_DK:(h + 1) * A_DK] * kscale).astype(BF16),
                v_aug=jnp.concatenate([v_ref[:, h * A_DV:(h + 1) * A_DV], ones_tile], axis=1),
                fc=f_col[:, cf:cf + 1], fr=f_row[cf:cf + 1, :], ir=g_row[ci:ci + 1, :], ic=g[:, ci:ci + 1],
                m_prev=m_ref[j:j + 1, 0:1], st=st_ref[j]))

    for grp in (probs,):
        for p in grp:
            p["qk"] = _mm_nt(p["q"], p["k"])
            p["q_st"] = _mm(p["q"], p["st"])
        for p in grp:
            d_log = jnp.where(p["incl"], p["fc"] - p["fr"] + p["ir"], -jnp.inf)
            inter = p["fc"] + p["m_prev"]
            p["m_t"] = jnp.maximum(inter, jnp.max(d_log, axis=-1, keepdims=True))
            p["s"] = p["qk"] * jnp.exp(d_log - p["m_t"])
            p["w_inter"] = jnp.exp(inter - p["m_t"])
        for p in grp:
            h = p["h"]
            num = _mm(p["s"], p["v_aug"]) + p["w_inter"] * p["q_st"]
            den = num[:, A_DV:A_DV + 1]
            p["h_ref"][:, h * A_DV:(h + 1) * A_DV] = (
                num[:, :A_DV] / jnp.maximum(jnp.abs(den), jnp.exp(-p["m_t"])))
        for p in grp:
            j, last = p["j"], p["last"]
            m_new = p["m_t"][last:last + 1, :]
            f_last = p["fc"][last:last + 1, :]
            w_k = jnp.exp(f_last - p["fc"] + p["ic"] - m_new)
            a = jnp.exp(f_last + p["m_prev"] - m_new)
            st_ref[j] = a * p["st"] + _mm_tn(p["k"], p["v_aug"] * w_k)
            m_ref[j:j + 1, :] = jnp.broadcast_to(m_new, (1, LANES))


def _chunk_rows(nchunk):
    return (lambda n: n), (lambda n: nchunk - 1 - n)


def _mlstm(proj, gate_bias):
    bsz, seq, _ = proj.shape
    fwd, bwd = _chunk_rows(seq // CHUNK)

    def specs(row):
        return [pl.BlockSpec((None, CHUNK, A_QK), lambda b, n: (b, row(n), OFF["aq"] // A_QK)),
                pl.BlockSpec((None, CHUNK, A_QK), lambda b, n: (b, row(n), OFF["ak"] // A_QK)),
                pl.BlockSpec((None, CHUNK, A_V), lambda b, n: (b, row(n), OFF["av"] // A_V)),
                pl.BlockSpec((None, CHUNK, LANES), lambda b, n: (b, row(n), OFF["gates"] // LANES))]

    nprob = 2 * A_HEADS
    return pl.pallas_call(
        _mlstm_kernel,
        grid=(bsz, seq // CHUNK),
        in_specs=specs(fwd) + specs(bwd) + [pl.BlockSpec((1, LANES), lambda b, n: (0, 0))],
        out_specs=[pl.BlockSpec((None, CHUNK, A_V), lambda b, n: (b, fwd(n), 0)),
                   pl.BlockSpec((None, CHUNK, A_V), lambda b, n: (b, bwd(n), 0))],
        out_shape=[jax.ShapeDtypeStruct((bsz, seq, A_V), F32)] * 2,
        scratch_shapes=[pltpu.VMEM((nprob, A_DK, A_AUG), F32), pltpu.VMEM((nprob, LANES), F32)],
        compiler_params=_cparams(("parallel", "arbitrary")),
        name="mlstm",
    )(proj, proj, proj, proj, proj, proj, proj, proj, gate_bias)


HALO = 8


def _conv_kernel(cur_ref, prev_ref, nxt_ref, w_ref, b_ref, o_ref, buf_ref):
    i = pl.program_id(1)
    tc = cur_ref.shape[0]
    first = i == 0
    last = i == pl.num_programs(1) - 1
    buf_ref[0:HALO, :] = jnp.where(first, 0.0, prev_ref[...])
    buf_ref[HALO:HALO + tc, :] = cur_ref[...]
    buf_ref[HALO + tc:HALO + tc + HALO, :] = jnp.where(last, 0.0, nxt_ref[...])
    pad = CONV_K // 2
    qscale = B_DK ** -0.5
    for c in range(B_QKV // LANES):
        cols = slice(c * LANES, (c + 1) * LANES)
        y = b_ref[:, cols]
        for j in range(CONV_K):
            y = y + w_ref[j:j + 1, cols] * buf_ref[HALO - pad + j:HALO - pad + j + tc, cols]
        y = y * _sigmoid(y)
        if c < 2 * B_HEADS:
            y = y * lax.rsqrt(jnp.sum(y * y, axis=-1, keepdims=True) + EPS)
            if c < B_HEADS:
                y = y * qscale
        o_ref[:, cols] = y


def _short_conv_qkv(proj, conv_w, conv_b, bsz, seq, tc=256):
    nt = seq // tc
    hb = tc // HALO
    nrow8 = bsz * seq // HALO
    cur = lambda b, i: (b * nt + i, 0)
    prev = lambda b, i: (jnp.maximum((b * nt + i) * hb - 1, 0), 0)
    nxt = lambda b, i: (jnp.minimum((b * nt + i + 1) * hb, nrow8 - 1), 0)
    w8 = jnp.zeros((8, B_QKV), F32).at[:CONV_K].set(conv_w)
    return pl.pallas_call(
        _conv_kernel,
        grid=(bsz, nt),
        in_specs=[pl.BlockSpec((tc, B_QKV), cur), pl.BlockSpec((HALO, B_QKV), prev),
                  pl.BlockSpec((HALO, B_QKV), nxt),
                  pl.BlockSpec((8, B_QKV), lambda b, i: (0, 0)), pl.BlockSpec((1, B_QKV), lambda b, i: (0, 0))],
        out_specs=pl.BlockSpec((tc, B_QKV), cur),
        out_shape=jax.ShapeDtypeStruct((bsz * seq, B_QKV), F32),
        scratch_shapes=[pltpu.VMEM((tc + 2 * HALO, B_QKV), F32)],
        compiler_params=_cparams(("parallel", "arbitrary")),
        name="short_conv",
    )(proj, proj, proj, w8, conv_b.reshape(1, B_QKV))


def _gdn_kernel(qkvf_ref, gf_ref, qkvb_ref, gb_ref, alog_ref, dtb_ref, of_ref, ob_ref, s_ref):
    n = pl.program_id(0)
    bsz = qkvf_ref.shape[0]

    @pl.when(n == 0)
    def _():
        s_ref[...] = jnp.zeros_like(s_ref)

    r_i = lax.broadcasted_iota(jnp.int32, (CHUNK, CHUNK), 0)
    c_i = lax.broadcasted_iota(jnp.int32, (CHUNK, CHUNK), 1)
    eye = jnp.where(r_i == c_i, 1.0, 0.0).astype(F32)

    probs = []
    for b, (d, (x_ref, g_ref, o_ref)) in itertools.product(
            range(bsz), enumerate(((qkvf_ref, gf_ref, of_ref), (qkvb_ref, gb_ref, ob_ref)))):
        rev = d == 1
        incl, strict = _tri_masks(rev)
        cum = jnp.where(incl, 1.0, 0.0).astype(BF16)
        gt = g_ref[b]
        dec = -jnp.exp(alog_ref[...]) * _softplus(gt + dtb_ref[...])
        beta = _sigmoid(gt)
        g_col = _mm_exact_lhs(cum, dec)
        g_row = g_col.T
        e_col = jnp.exp(g_col)
        for h in range(B_HEADS):
            cd = GDN_GATE_LANE + d * B_HEADS + h
            cb = GDN_GATE_LANE + 2 * B_HEADS + d * B_HEADS + h
            probs.append(dict(
                j=(b * 2 + d) * B_HEADS + h, b=b, h=h, o_ref=o_ref, incl=incl, strict=strict,
                last=0 if rev else CHUNK - 1,
                q=x_ref[b, :, h * B_DK:(h + 1) * B_DK].astype(BF16),
                k=x_ref[b, :, B_QK + h * B_DK:B_QK + (h + 1) * B_DK],
                v=x_ref[b, :, 2 * B_QK + h * B_DV:2 * B_QK + (h + 1) * B_DV],
                gc=g_col[:, cd:cd + 1], gr=g_row[cd:cd + 1, :], ec=e_col[:, cd:cd + 1], bc=beta[:, cb:cb + 1]))

    for p in probs:
        kb = p["k"].astype(BF16)
        p["kq"] = _mm_nt(jnp.concatenate([kb, p["q"]], axis=0), kb)
    for p in probs:
        e = jnp.exp(jnp.where(p["incl"], p["gc"] - p["gr"], -jnp.inf))
        p["a_qk"] = p["kq"][CHUNK:] * e
        p["lmat"] = (p["bc"] * p["kq"][:CHUNK]) * jnp.where(p["strict"], e, 0.0)
    same_block = lambda s: (r_i // s) == (c_i // s)
    for p in probs:
        p["t"] = eye - jnp.where(same_block(2), p["lmat"], 0.0)
    s = 2
    while s < CHUNK:
        couple = same_block(2 * s) & jnp.logical_not(same_block(s))
        for p in probs:
            p["cd"] = _mm(jnp.where(couple, p["lmat"], 0.0), p["t"])
        for p in probs:
            p["t"] = p["t"] - _mm(p["t"], p["cd"])
        s *= 2
    for p in probs:
        rhs = jnp.concatenate([(p["bc"] * p["ec"]) * p["k"], p["bc"] * p["v"]], axis=1)
        p["wu"] = _mm(p["t"], rhs)
    for p in probs:
        p["s"] = s_ref[p["j"]]
        p["wq_s"] = _mm(jnp.concatenate([p["wu"][:, :B_DK].astype(BF16), p["q"]], axis=0), p["s"])
    for p in probs:
        p["uu"] = p["wu"][:, B_DK:] - p["wq_s"][:CHUNK]
    for p in probs:
        h = p["h"]
        p["o_ref"][p["b"], :, h * B_DV:(h + 1) * B_DV] = p["ec"] * p["wq_s"][CHUNK:] + _mm(p["a_qk"], p["uu"])
    for p in probs:
        g_last = p["gc"][p["last"]:p["last"] + 1, :]
        k_dec = p["k"] * jnp.exp(g_last - p["gc"])
        s_ref[p["j"]] = jnp.exp(g_last) * p["s"] + _mm_tn(k_dec, p["uu"])


def _gdn(qkv, proj, a_log, dt_bias):
    bsz, seq, _ = proj.shape
    fwd, bwd = _chunk_rows(seq // CHUNK)
    vec = lambda: pl.BlockSpec((1, LANES), lambda n: (0, 0))
    return pl.pallas_call(
        _gdn_kernel,
        grid=(seq // CHUNK,),
        in_specs=[pl.BlockSpec((bsz, CHUNK, B_QKV), lambda n: (0, fwd(n), 0)),
                  pl.BlockSpec((bsz, CHUNK, LANES), lambda n: (0, fwd(n), OFF["gates"] // LANES)),
                  pl.BlockSpec((bsz, CHUNK, B_QKV), lambda n: (0, bwd(n), 0)),
                  pl.BlockSpec((bsz, CHUNK, LANES), lambda n: (0, bwd(n), OFF["gates"] // LANES)),
                  vec(), vec()],
        out_specs=[pl.BlockSpec((bsz, CHUNK, B_V), lambda n: (0, fwd(n), 0)),
                   pl.BlockSpec((bsz, CHUNK, B_V), lambda n: (0, bwd(n), 0))],
        out_shape=[jax.ShapeDtypeStruct((bsz, seq, B_V), F32)] * 2,
        scratch_shapes=[pltpu.VMEM((bsz * 2 * B_HEADS, B_DK, B_DV), F32)],
        compiler_params=_cparams(("arbitrary",)),
        name="gdn",
    )(qkv, proj, qkv, proj, a_log, dt_bias)


SUB = 16


def _gla_kernel(qf_ref, kf_ref, vf_ref, rf_ref, qb_ref, kb_ref, vb_ref, rb_ref, wg_ref, bg_ref,
                of_ref, ob_ref, s_ref):
    n = pl.program_id(0)
    bsz = qf_ref.shape[0]

    @pl.when(n == 0)
    def _():
        s_ref[...] = jnp.zeros_like(s_ref)

    qscale = C_DK ** -0.5
    row = lax.broadcasted_iota(jnp.int32, (CHUNK, C_DK), 0)
    nsub = CHUNK // SUB
    probs = []
    for b, (d, (q_ref, k_ref, v_ref, r_ref, o_ref)) in itertools.product(range(bsz), enumerate(
            ((qf_ref, kf_ref, vf_ref, rf_ref, of_ref), (qb_ref, kb_ref, vb_ref, rb_ref, ob_ref)))):
        rev = d == 1
        incl, _ = _tri_masks(rev)
        cum = jnp.where(incl, 1.0, 0.0).astype(BF16)
        z = _mm_x3(r_ref[b], wg_ref[:, d * C_QK:(d + 1) * C_QK]) + bg_ref[:, d * C_QK:(d + 1) * C_QK]
        lg = _log_sigmoid(z) / GLA_TAU
        b_all = _mm_exact_lhs(cum, lg)
        for h in range(C_HEADS):
            j = (b * 2 + d) * C_HEADS + h
            probs.append(dict(
                j=j, bi=b, h=h, o_ref=o_ref, rev=rev, incl=incl, last=0 if rev else CHUNK - 1,
                q=q_ref[b, :, h * C_DK:(h + 1) * C_DK] * qscale, k=k_ref[b, :, h * C_DK:(h + 1) * C_DK],
                v=v_ref[b, :, h * C_DV:(h + 1) * C_DV].astype(BF16), b=b_all[:, h * C_DK:(h + 1) * C_DK],
                s_t=s_ref[j]))

    for p in probs:
        q, k, b, rev = p["q"], p["k"], p["b"], p["rev"]
        k_slots, in_blocks = [], []
        b_ref_rows = b
        for i in range(nsub):
            ref_row = (i + 1) * SUB - 1 if rev else i * SUB
            b_ref_row = b[ref_row:ref_row + 1, :]
            in_blocks.append((row >= i * SUB) & (row < (i + 1) * SUB))
            b_ref_rows = jnp.where(in_blocks[i], b_ref_row, b_ref_rows)
            visible = (row >= i * SUB) if rev else (row < (i + 1) * SUB)
            k_slots.append((k * jnp.exp(jnp.where(visible, b_ref_row - b, -jnp.inf))).astype(BF16))
        q_dec = q * jnp.exp(b - b_ref_rows)
        q_slots = [jnp.where(m, q_dec, 0.0).astype(BF16) for m in in_blocks]
        p["att"] = _mm_nt(jnp.concatenate(q_slots, axis=1), jnp.concatenate(k_slots, axis=1))
        p["q_s"] = _mm_nt(q * jnp.exp(b), p["s_t"])
    for p in probs:
        h = p["h"]
        att = jnp.where(p["incl"], p["att"], 0.0)
        p["o_ref"][p["bi"], :, h * C_DV:(h + 1) * C_DV] = _mm(att, p["v"]) + p["q_s"]
    for p in probs:
        b, last = p["b"], p["last"]
        b_last = b[last:last + 1, :]
        s_ref[p["j"]] = jnp.exp(b_last) * p["s_t"] + _mm_tn(p["v"], p["k"] * jnp.exp(b_last - b))


def _gla(proj, w_gate, b_gate):
    bsz, seq, _ = proj.shape
    fwd, bwd = _chunk_rows(seq // CHUNK)

    def specs(row):
        return [pl.BlockSpec((bsz, CHUNK, C_QK), lambda n: (0, row(n), OFF["cq"] // C_QK)),
                pl.BlockSpec((bsz, CHUNK, C_QK), lambda n: (0, row(n), OFF["ck"] // C_QK)),
                pl.BlockSpec((bsz, CHUNK, C_V), lambda n: (0, row(n), OFF["cv"] // C_V)),
                pl.BlockSpec((bsz, CHUNK, LANES), lambda n: (0, row(n), OFF["gates"] // LANES))]

    return pl.pallas_call(
        _gla_kernel,
        grid=(seq // CHUNK,),
        in_specs=specs(fwd) + specs(bwd) + [pl.BlockSpec((LANES, 2 * C_QK), lambda n: (0, 0)),
                                            pl.BlockSpec((1, 2 * C_QK), lambda n: (0, 0))],
        out_specs=[pl.BlockSpec((bsz, CHUNK, C_V), lambda n: (0, fwd(n), 0)),
                   pl.BlockSpec((bsz, CHUNK, C_V), lambda n: (0, bwd(n), 0))],
        out_shape=[jax.ShapeDtypeStruct((bsz, seq, C_V), F32)] * 2,
        scratch_shapes=[pltpu.VMEM((bsz * 2 * C_HEADS, C_DV, C_DK), F32)],
        compiler_params=_cparams(("arbitrary",)),
        name="gla",
    )(proj, proj, proj, proj, proj, proj, proj, proj, w_gate, b_gate)


def _post_kernel(hf_ref, hb_ref, z_ref, g_ref, o_ref, *, head_dim, gate):
    nheads = hf_ref.shape[1] // head_dim
    for h in range(nheads):
        cols = slice(h * head_dim, (h + 1) * head_dim)
        x = hf_ref[:, cols] + hb_ref[:, cols]
        y = x * lax.rsqrt(jnp.mean(x * x, axis=-1, keepdims=True) + EPS) * g_ref[:, cols]
        z = z_ref[:, cols]
        sg = _sigmoid(z)
        o_ref[:, cols] = (y * (sg if gate == "sigmoid" else z * sg)).astype(o_ref.dtype)


def _post(hf, hb, proj, z_off, gain, head_dim, gate, tm=256):
    m, w = hf.shape
    return pl.pallas_call(
        functools.partial(_post_kernel, head_dim=head_dim, gate=gate),
        grid=(m // tm,),
        in_specs=[pl.BlockSpec((tm, w), lambda i: (i, 0)), pl.BlockSpec((tm, w), lambda i: (i, 0)),
                  pl.BlockSpec((tm, w), lambda i: (i, z_off // w)), pl.BlockSpec((1, w), lambda i: (0, 0))],
        out_specs=pl.BlockSpec((tm, w), lambda i: (i, 0)),
        out_shape=jax.ShapeDtypeStruct((m, w), BF16),
        compiler_params=_cparams(("parallel",)),
        name="head_norm_gate",
    )(hf, hb, proj, gain.reshape(1, w))


def _merge_kernel(ya_ref, yb_ref, yc_ref, gh_ref, wa_ref, wb_ref, wc_ref, wg_ref, bg_ref, o_ref):
    gh = gh_ref[...].astype(BF16)
    acc = None
    for j, (y_ref, w_ref) in enumerate(((ya_ref, wa_ref), (yb_ref, wb_ref), (yc_ref, wc_ref))):
        gate = _sigmoid(jnp.dot(gh, wg_ref[j], preferred_element_type=F32) + bg_ref[j])
        term = gate * jnp.dot(y_ref[...], w_ref[...], preferred_element_type=F32)
        acc = term if acc is None else acc + term
    o_ref[...] = acc.astype(o_ref.dtype)


def _merge(ya, yb, yc, proj, w_branch, w_gate, b_gate, layer, tm=512, tn=512):
    m, kb = ya.shape
    n = w_gate.shape[-1]
    ysp = lambda: pl.BlockSpec((tm, kb), lambda i, j: (i, 0))
    wsp = lambda: pl.BlockSpec((None, kb, tn), lambda i, j: (layer, 0, j))
    return pl.pallas_call(
        _merge_kernel,
        grid=(m // tm, n // tn),
        in_specs=[ysp(), ysp(), ysp(),
                  pl.BlockSpec((tm, GATE_RANK), lambda i, j: (i, OFF["gh"] // GATE_RANK)),
                  wsp(), wsp(), wsp(),
                  pl.BlockSpec((None, 3, GATE_RANK, tn), lambda i, j: (layer, 0, 0, j)),
                  pl.BlockSpec((None, 3, 1, tn), lambda i, j: (layer, 0, 0, j))],
        out_specs=pl.BlockSpec((tm, tn), lambda i, j: (i, j)),
        out_shape=jax.ShapeDtypeStruct((m, n), BF16),
        compiler_params=_cparams(("parallel", "parallel")),
        name="merge",
    )(ya, yb, yc, proj, *w_branch, w_gate, b_gate)


def _cast_kernel(x_ref, o_ref):
    o_ref[...] = x_ref[...].astype(o_ref.dtype)


def _scale_cast_kernel(x_ref, g_ref, o_ref):
    o_ref[...] = (x_ref[...] * g_ref[...]).astype(o_ref.dtype)


def _cast_bf16(w, row_gain=None):
    shape = w.shape
    lead, (rows, cols) = shape[:-2], shape[-2:]
    tr, tc = _pick_tile(rows, (1024, 512, 256, 128, 64, 32, 16)), _pick_tile(cols, (2048, 1024, 512, 256, 128))
    nl = len(lead)
    spec = pl.BlockSpec((None,) * nl + (tr, tc), lambda *g: g)
    in_specs, args, body = [spec], [w], _cast_kernel
    if row_gain is not None:
        in_specs.append(pl.BlockSpec((None,) * nl + (tr, 1), lambda *g: g[:-1] + (0,)))
        args.append(row_gain.astype(F32).reshape(lead + (rows, 1)))
        body = _scale_cast_kernel
    return pl.pallas_call(
        body,
        grid=lead + (rows // tr, cols // tc),
        in_specs=in_specs,
        out_specs=spec,
        out_shape=jax.ShapeDtypeStruct(shape, BF16),
        compiler_params=_cparams(("parallel",) * (nl + 2)),
        name="cast_bf16",
    )(*args)


def _permute_w_in_kernel(x_ref, g_ref, o_ref):
    gain = g_ref[...]
    for (start, w), name in zip(_PIECES, _NEW_ORDER):
        o_ref[OFF[name]:OFF[name] + w, :] = (x_ref[start:start + w, :] * gain).astype(BF16)
    o_ref[_PROJ_USED:, :] = jnp.zeros((PROJ_W - _PROJ_USED, x_ref.shape[1]), BF16)


def _permute_w_in(w_in, gain):
    depth, d, width = w_in.shape
    w_t = jnp.swapaxes(w_in, 1, 2)
    tk = _pick_tile(d, (256, 128))
    return pl.pallas_call(
        _permute_w_in_kernel,
        grid=(depth, d // tk),
        in_specs=[pl.BlockSpec((None, width, tk), lambda l, i: (l, 0, i)),
                  pl.BlockSpec((None, 1, tk), lambda l, i: (l, 0, i))],
        out_specs=pl.BlockSpec((None, PROJ_W, tk), lambda l, i: (l, 0, i)),
        out_shape=jax.ShapeDtypeStruct((depth, PROJ_W, d), BF16),
        compiler_params=_cparams(("parallel", "parallel")),
        name="permute_w_in",
    )(w_t, gain.astype(F32).reshape(depth, 1, d))


def _pad_lanes(v, lane0):
    return jnp.zeros(v.shape[:-1] + (LANES,), F32).at[..., lane0:lane0 + v.shape[-1]].set(v.astype(F32))


def _gla_gate_weights(w, b):
    depth = w.shape[0]
    wg = jnp.zeros((depth, LANES, 2 * C_QK), F32)
    for n in range(2):
        r0 = GLA_GATE_LANE + n * GLA_RANK
        wg = wg.at[:, r0:r0 + GLA_RANK, n * C_QK:(n + 1) * C_QK].set(w[:, n].astype(F32))
    return wg, b.astype(F32).reshape(depth, 1, 2 * C_QK)


def _pick_tile(n, prefs):
    for t in prefs:
        if n % t == 0:
            return t
    return n


def kernel(x, norm1_g, w_in, conv_w, conv_b, mlstm_gate_b, mlstm_norm_g, gdn_a_log, gdn_dt_bias, gdn_norm_g, gla_w_gate, gla_b_gate, gla_norm_g, w_branch_a, w_branch_b, w_branch_c, w_merge_gate, b_merge_gate, w_out, norm2_g, w_ff1, w_ff2, final_g):
    bsz, seq, d_model = x.shape
    depth = w_in.shape[0]
    m = bsz * seq
    assert seq % CHUNK == 0 and m % 256 == 0

    w_in_p = _permute_w_in(w_in, norm1_g)
    w_br = [_cast_bf16(w) for w in (w_branch_a, w_branch_b, w_branch_c)]
    w_mg = _cast_bf16(w_merge_gate)
    b_mg = b_merge_gate.astype(F32).reshape(depth, 3, 1, d_model)
    w_out_b = _cast_bf16(w_out)
    w_ff1_b = _cast_bf16(w_ff1, row_gain=norm2_g)
    w_ff2_b = _cast_bf16(w_ff2)
    gate_bias_a = _pad_lanes(mlstm_gate_b.reshape(depth, 1, 4 * A_HEADS), MLSTM_GATE_LANE)
    a_log = _pad_lanes(gdn_a_log.reshape(depth, 1, 2 * B_HEADS), GDN_GATE_LANE)
    dt_bias = _pad_lanes(gdn_dt_bias.reshape(depth, 1, 2 * B_HEADS), GDN_GATE_LANE)
    gla_wg, gla_bg = _gla_gate_weights(gla_w_gate, gla_b_gate)

    tm = _pick_tile(m, (1024, 512, 256))
    d_ff = w_ff1.shape[-1]
    xs = x.reshape(m, d_model).astype(F32)
    xb, ssq = _stream_init(xs)
    for l in range(depth):
        proj = _matmul_normed(xb, ssq, w_in_p, l, tm=tm, tn=PROJ_TILE, out_dtype=F32, b_transposed=True,
                              name="in_proj")

        proj3 = proj.reshape(bsz, seq, PROJ_W)
        flat = lambda hs: [h.reshape(m, h.shape[-1]) for h in hs]
        ha_f, ha_b = flat(_mlstm(proj3, gate_bias_a[l]))
        qkv = _short_conv_qkv(proj, conv_w[l], conv_b[l], bsz, seq)
        hb_f, hb_b = flat(_gdn(qkv.reshape(bsz, seq, B_QKV), proj3, a_log[l], dt_bias[l]))
        hc_f, hc_b = flat(_gla(proj3, gla_wg[l], gla_bg[l]))

        ya = _post(ha_f, ha_b, proj, OFF["ao"], mlstm_norm_g[l], A_DV, "sigmoid")
        yb = _post(hb_f, hb_b, proj, OFF["bz"], gdn_norm_g[l], B_DV, "silu")
        yc = _post(hc_f, hc_b, proj, OFF["cr"], gla_norm_g[l], C_DV, "silu")
        mix = _merge(ya, yb, yc, proj, w_br, w_mg, b_mg, l, tm=tm, tn=_pick_tile(d_model, (512, 256, 128)))
        xs, xb, ssq = _matmul_residual_stream(mix, w_out_b, l, xs, tm=tm, tn=_pick_tile(d_model, (512, 256, 128)),
                                              name="out_proj")

        hid = _matmul_normed(xb, ssq, w_ff1_b, l, tm=tm, tn=_pick_tile(d_ff, (1024, 512, 256, 128)),
                             out_dtype=BF16, epilogue="relu2", name="ff1")
        xs, xb, ssq = _matmul_ktiled_residual(hid, w_ff2_b, l, xs, tm=tm,
                                              tn=_pick_tile(d_model, (1024, 512, 256, 128)),
                                              tk=_pick_tile(d_ff, (2048, 1024, 512, 256, 128)), name="ff2")
    out = _rmsnorm(xs, final_g, x.dtype)
    return out.reshape(bsz, seq, d_model)
```
